```python
import math, functools
import jax, jax.numpy as jnp
from jax import lax
import numpy as np

D_MODEL = 4096
BATCH = 1
SEQ = 8192
DEPTH = 1
DEC_BATCH = 128
DEC_SEQ = 1
PAST_LEN = 2048
PAGE_SIZE = 128

DK_DA = 128
DV_DA = 2 * DK_DA
H_DA = D_MODEL // 2 // DV_DA
W_DA = H_DA * DV_DA
DK_RET = 128
DV_RET = 256
H_RET = D_MODEL // 2 // DV_RET
W_RET = H_RET * DV_RET
N_DQ = H_DA * 2 * DK_DA
N_DK = H_DA * 2 * DK_DA
N_DV = W_DA
N_RQ = H_RET * DK_RET
N_RK = H_RET * DK_RET
N_RV = W_RET
N_RG = W_RET
D_IN = N_DQ + N_DK + N_DV + N_RQ + N_RK + N_RV + N_RG
D_MIX = W_DA + W_RET
ROPE_THETA = 10000.0
RET_THETA = 10000.0
Q_BLOCK = 128
RET_CHUNK = 128
N_EXPERTS = 32
TOP_K = 4
D_FF = D_MODEL
SWIGLU_LIMIT = 7.0
SWIGLU_ALPHA = 1.702
MOE_BLOCK = 128
NORM_EPS = 1e-6
NEG_INF = -1e30
N_ADA = 6

kernel_name = "hymba_diffattn_retnet_moe_adaln_step"

F32 = jnp.float32


def rms_norm(x, g):
    xf = x.astype(F32)
    y = xf * lax.rsqrt(jnp.mean(xf * xf, axis=-1, keepdims=True) + NORM_EPS)
    return (y * g.astype(F32)).astype(x.dtype)


def rope_half(x, pos):
    half = x.shape[-1] // 2
    inv = ROPE_THETA ** (-jnp.arange(half, dtype=F32) / half)
    ang = pos.astype(F32)[:, None] * inv[None, :]
    shape = (1, pos.shape[0]) + (1,) * (x.ndim - 3) + (half,)
    cos, sin = jnp.cos(ang).reshape(shape), jnp.sin(ang).reshape(shape)
    xf = x.astype(F32)
    x1, x2 = xf[..., :half], xf[..., half:]
    return jnp.concatenate([x1 * cos - x2 * sin, x2 * cos + x1 * sin], axis=-1).astype(x.dtype)


def ret_rotate(x, pos):
    half = x.shape[-1] // 2
    inv = 1.0 / (RET_THETA ** jnp.linspace(0.0, 1.0, half, dtype=F32))
    ang = pos.astype(F32)[:, None] * inv[None, :]
    cos = jnp.cos(ang)[None, :, None, :]
    sin = jnp.sin(ang)[None, :, None, :]
    xf = x.astype(F32)
    xe, xo = xf[..., 0::2], xf[..., 1::2]
    out = jnp.stack([xe * cos - xo * sin, xo * cos + xe * sin], axis=-1).reshape(x.shape)
    return out.astype(x.dtype)


def ret_log_decay():
    return jnp.log1p(-jnp.exp2(-5.0 - jnp.arange(H_RET, dtype=F32)))


def project_mix(h, pos, w_in, g_qn, g_kn):
    B, S, _ = h.shape
    z = jnp.dot(h, w_in)
    bounds = [int(b) for b in np.cumsum([N_DQ, N_DK, N_DV, N_RQ, N_RK, N_RV])]
    dq, dk, dv, rq, rk, rv, rg = jnp.split(z, bounds, axis=-1)
    dq = rope_half(rms_norm(dq.reshape(B, S, H_DA, 2, DK_DA), g_qn), pos)
    dk = rope_half(rms_norm(dk.reshape(B, S, H_DA, 2, DK_DA), g_kn), pos)
    dv = dv.reshape(B, S, H_DA, DV_DA)
    rq = ret_rotate(rq.reshape(B, S, H_RET, DK_RET), pos)
    rk = ret_rotate(rk.reshape(B, S, H_RET, DK_RET), pos) * (DK_RET ** -0.5)
    rv = rv.reshape(B, S, H_RET, DV_RET)
    return dq, dk, dv, rq, rk, rv, rg.reshape(B, S, H_RET, DV_RET)


def diff_attn_prompt(q, k, v, lam):
    B, S, H, _, Dk = q.shape
    nb = S // Q_BLOCK
    scale = Dk ** -0.5
    qb = q.reshape(B, nb, Q_BLOCK, H, 2, Dk).swapaxes(0, 1)
    kpos = jnp.arange(S)
    vf = v.astype(F32)

    def one_block(args):
        qi, bi = args
        s = jnp.einsum('bqhmd,bkhmd->bhmqk', qi, k, preferred_element_type=F32) * scale
        qpos = bi * Q_BLOCK + jnp.arange(Q_BLOCK)
        s = jnp.where(kpos[None, :] <= qpos[:, None], s, NEG_INF)
        p = jax.nn.softmax(s, axis=-1)
        a = p[:, :, 0] - lam * p[:, :, 1]
        return jnp.einsum('bhqk,bkhd->bqhd', a, vf)

    o = lax.map(one_block, (qb, jnp.arange(nb)))
    return o.swapaxes(0, 1).reshape(B, S, H, v.shape[-1]).astype(q.dtype)


def _online_update(carry, s, vb):
    m, l, acc = carry
    m_new = jnp.maximum(m, jnp.max(s, axis=-1))
    alpha = jnp.exp(m - m_new)
    p = jnp.exp(s - m_new[..., None])
    l = l * alpha + jnp.sum(p, axis=-1)
    acc = acc * alpha[..., None] + jnp.einsum('bhmqk,bkhd->bhmqd', p, vb.astype(F32))
    return (m_new, l, acc)


def diff_attn_sample(q, k, v, lam, cache_k, cache_v, page_table, layer):
    Bd, Sd, H, _, Dk = q.shape
    Dv = v.shape[-1]
    scale = Dk ** -0.5

    def scores(kb):
        return jnp.einsum('bqhmd,bkhmd->bhmqk', q, kb, preferred_element_type=F32) * scale

    def page_step(carry, pages):
        kb = cache_k[layer, pages]
        vb = cache_v[layer, pages]
        return _online_update(carry, scores(kb), vb), None

    init = (jnp.full((Bd, H, 2, Sd), NEG_INF, F32), jnp.zeros((Bd, H, 2, Sd), F32),
            jnp.zeros((Bd, H, 2, Sd, Dv), F32))
    carry, _ = lax.scan(page_step, init, page_table.T)
    qi = jnp.arange(Sd)
    s_new = jnp.where(qi[None, :] <= qi[:, None], scores(k), NEG_INF)
    m, l, acc = _online_update(carry, s_new, v)
    o = acc / l[..., None]
    out = o[:, :, 0] - lam * o[:, :, 1]
    return out.transpose(0, 2, 1, 3).astype(q.dtype)


def retention_chunk(state, q, k, v, lg):
    C = q.shape[1]
    q, k, v = q.astype(F32), k.astype(F32), v.astype(F32)
    state = state.astype(F32)
    idx = jnp.arange(C, dtype=F32)
    diff = idx[:, None] - idx[None, :]
    causal = diff >= 0
    dmask = jnp.where(causal, jnp.exp(jnp.where(causal, diff, 0.0)[None] * lg[:, None, None]), 0.0)
    qk = jnp.einsum('bihd,bjhd->bhij', q, k) * dmask[None]
    inner = jnp.einsum('bhij,bjhe->bihe', qk, v)
    qdec = jnp.exp((idx + 1.0)[:, None] * lg[None, :])
    cross = jnp.einsum('bihd,bhde->bihe', q, state) * qdec[None, :, :, None]
    kdec = jnp.exp((C - 1.0 - idx)[:, None] * lg[None, :])
    new_state = (jnp.exp(C * lg)[None, :, None, None] * state
                 + jnp.einsum('bjhd,jh,bjhe->bhde', k, kdec, v))
    return inner + cross, new_state


def retention_prompt(q, k, v, lg):
    B, S, H, Dk = q.shape
    Dv = v.shape[-1]
    n = S // RET_CHUNK

    def chunks(a):
        return a.reshape((B, n, RET_CHUNK) + a.shape[2:]).swapaxes(0, 1)

    def step(st, qkv):
        o, st = retention_chunk(st, qkv[0], qkv[1], qkv[2], lg)
        return st, o

    st0 = jnp.zeros((B, H, Dk, Dv), F32)
    st, o = lax.scan(step, st0, (chunks(q), chunks(k), chunks(v)))
    return o.swapaxes(0, 1).reshape(B, S, H, Dv), st


def moe(h, layer, w_router, b_router, w_gu, b_gu, w_down, b_down):
    T, D = h.shape
    E = N_EXPERTS
    logits = jnp.dot(h, w_router, preferred_element_type=F32) + b_router
    top_v, top_i = lax.top_k(logits, TOP_K)
    gates = jax.nn.softmax(top_v, axis=-1)
    A = T * TOP_K
    flat_e = top_i.reshape(A)
    flat_t = jnp.repeat(jnp.arange(T, dtype=jnp.int32), TOP_K)
    flat_g = gates.reshape(A)
    order = jnp.argsort(flat_e)
    se = flat_e[order]
    counts = jnp.bincount(flat_e, length=E)
    start = jnp.cumsum(counts) - counts
    pcounts = (counts + MOE_BLOCK - 1) // MOE_BLOCK * MOE_BLOCK
    pend = jnp.cumsum(pcounts)
    pstart = pend - pcounts
    dest = pstart[se] + jnp.arange(A) - start[se]
    n_blocks = (A + E * (MOE_BLOCK - 1) + MOE_BLOCK - 1) // MOE_BLOCK
    P = n_blocks * MOE_BLOCK
    row_tok = jnp.full((P,), T, jnp.int32).at[dest].set(flat_t[order])
    row_gate = jnp.zeros((P,), F32).at[dest].set(flat_g[order])
    blk_e = jnp.minimum(jnp.searchsorted(pend, jnp.arange(n_blocks) * MOE_BLOCK, side='right'), E - 1)
    h_pad = jnp.concatenate([h, jnp.zeros((1, D), h.dtype)], axis=0)
    xb = h_pad[row_tok].reshape(n_blocks, MOE_BLOCK, D)

    def expert_block(args):
        xi, e = args
        gu = jnp.dot(xi, w_gu[layer, e]) + b_gu[e]
        g = jnp.minimum(gu[:, 0::2], SWIGLU_LIMIT)
        u = jnp.clip(gu[:, 1::2], -SWIGLU_LIMIT, SWIGLU_LIMIT)
        act = (u + 1.0) * g * jax.nn.sigmoid(SWIGLU_ALPHA * g)
        return jnp.dot(act, w_down[layer, e]) + b_down[e]

    yb = lax.map(expert_block, (xb, blk_e)).reshape(P, D)
    out = jnp.zeros((T + 1, D), F32).at[row_tok].add(yb.astype(F32) * row_gate[:, None])
    return out[:T].astype(h.dtype)


def trunk_layer(x, c, pos, attn_core, ret_core, layer, lam, lam_init,
                w_ada, b_ada, g_attn, w_in, g_qn, g_kn, g_sub, g_ret, w_out,
                g_ffn, w_router, b_router, w_gu, b_gu, w_down, b_down):
    B, S, D = x.shape
    mod = jnp.dot(jax.nn.silu(c), w_ada[layer]) + b_ada[layer]
    sh1, sc1, gt1, sh2, sc2, gt2 = jnp.split(mod[:, None, :], N_ADA, axis=-1)
    h = rms_norm(x, g_attn[layer]) * (1.0 + sc1) + sh1
    dq, dk, dv, rq, rk, rv, rg = project_mix(h, pos, w_in[layer], g_qn[layer], g_kn[layer])
    o_da = attn_core(dq, dk, dv, lam)
    o_ret, st = ret_core(rq, rk, rv)
    o_da = rms_norm(o_da, g_sub[layer]) * (1.0 - lam_init)
    o_ret = (rms_norm(o_ret, g_ret[layer]) * jax.nn.silu(rg.astype(F32))).astype(x.dtype)
    o = jnp.concatenate([o_da.reshape(B, S, W_DA), o_ret.reshape(B, S, W_RET)], axis=-1)
    x = x + gt1 * jnp.dot(o, w_out[layer])
    h = rms_norm(x, g_ffn[layer]) * (1.0 + sc2) + sh2
    y = moe(h.reshape(B * S, D), layer, w_router[layer], b_router[layer], w_gu, b_gu[layer],
            w_down, b_down[layer])
    x = x + gt2 * y.reshape(B, S, D)
    return x, dk, dv, st


def setup_inputs(seed: int = 0) -> dict:
    key = jax.random.key(seed)
    ks = jax.random.split(key, 32)
    n_pages = PAST_LEN // PAGE_SIZE
    n_used = DEC_BATCH * n_pages
    n_pool = n_used + max(1, n_used // 4)

    def nrm(k, shape, scale=1.0):
        return jax.random.normal(k, shape, F32) * scale

    page_table = jax.random.permutation(ks[0], n_pool)[:n_used].reshape(DEC_BATCH, n_pages).astype(jnp.int32)
    return {
        'x_prompt': nrm(ks[1], (BATCH, SEQ, D_MODEL)),
        'x_sample': nrm(ks[2], (DEC_BATCH, DEC_SEQ, D_MODEL)),
        'cache_k': nrm(ks[3], (DEPTH, n_pool, PAGE_SIZE, H_DA, 2, DK_DA)),
        'cache_v': nrm(ks[4], (DEPTH, n_pool, PAGE_SIZE, H_DA, DV_DA)),
        'state_ret': nrm(ks[5], (DEPTH, DEC_BATCH, H_RET, DK_RET, DV_RET)),
        'page_table': page_table,
        'c_prompt': nrm(ks[6], (BATCH, D_MODEL)),
        'c_sample': nrm(ks[7], (DEC_BATCH, D_MODEL)),
        'w_ada': nrm(ks[8], (DEPTH, D_MODEL, N_ADA * D_MODEL), 0.5 * D_MODEL ** -0.5),
        'b_ada': nrm(ks[9], (DEPTH, N_ADA * D_MODEL), 0.02),
        'g_attn': 1.0 + nrm(ks[10], (DEPTH, D_MODEL), 0.02),
        'w_in': nrm(ks[11], (DEPTH, D_MODEL, D_IN), D_MODEL ** -0.5),
        'g_qn': 1.0 + nrm(ks[12], (DEPTH, DK_DA), 0.02),
        'g_kn': 1.0 + nrm(ks[13], (DEPTH, DK_DA), 0.02),
        'lam_q': nrm(ks[14], (DEPTH, 2, DK_DA), 0.1),
        'lam_k': nrm(ks[15], (DEPTH, 2, DK_DA), 0.1),
        'g_sub': 1.0 + nrm(ks[16], (DEPTH, DV_DA), 0.02),
        'g_ret': 1.0 + nrm(ks[17], (DEPTH, DV_RET), 0.02),
        'w_out': nrm(ks[18], (DEPTH, D_MIX, D_MODEL), D_MIX ** -0.5),
        'g_ffn': 1.0 + nrm(ks[19], (DEPTH, D_MODEL), 0.02),
        'w_router': nrm(ks[20], (DEPTH, D_MODEL, N_EXPERTS), D_MODEL ** -0.5),
        'b_router': nrm(ks[21], (DEPTH, N_EXPERTS), 0.01),
        'w_gu': nrm(ks[22], (DEPTH, N_EXPERTS, D_MODEL, 2 * D_FF), D_MODEL ** -0.5),
        'b_gu': nrm(ks[23], (DEPTH, N_EXPERTS, 2 * D_FF), 0.02),
        'w_down': nrm(ks[24], (DEPTH, N_EXPERTS, D_FF, D_MODEL), D_FF ** -0.5),
        'b_down': nrm(ks[25], (DEPTH, N_EXPERTS, D_MODEL), 0.02),
    }


def reference(x_prompt, x_sample, cache_k, cache_v, state_ret, page_table, c_prompt, c_sample,
              w_ada, b_ada, g_attn, w_in, g_qn, g_kn, lam_q, lam_k, g_sub, g_ret, w_out,
              g_ffn, w_router, b_router, w_gu, b_gu, w_down, b_down):
    past = page_table.shape[1] * cache_k.shape[2]
    pos_p = jnp.arange(x_prompt.shape[1])
    pos_s = past + jnp.arange(x_sample.shape[1])
    lg = ret_log_decay()
    weights = (w_ada, b_ada, g_attn, w_in, g_qn, g_kn, g_sub, g_ret, w_out,
               g_ffn, w_router, b_router, w_gu, b_gu, w_down, b_down)
    xp, xs = x_prompt, x_sample
    kp_l, vp_l, sp_l, ks_l, vs_l, ss_l = [], [], [], [], [], []
    for layer in range(DEPTH):
        lam_init = 0.8 - 0.6 * math.exp(-0.3 * layer)
        lam = (jnp.exp(jnp.sum(lam_q[layer, 0].astype(F32) * lam_k[layer, 0].astype(F32)))
               - jnp.exp(jnp.sum(lam_q[layer, 1].astype(F32) * lam_k[layer, 1].astype(F32))) + lam_init)
        ret_p = functools.partial(retention_prompt, lg=lg)
        xp, kp, vp, sp = trunk_layer(xp, c_prompt, pos_p, diff_attn_prompt, ret_p,
                                     layer, lam, lam_init, *weights)
        attn_s = functools.partial(diff_attn_sample, cache_k=cache_k, cache_v=cache_v,
                                   page_table=page_table, layer=layer)
        ret_s = functools.partial(retention_chunk, state_ret[layer], lg=lg)
        xs, ksn, vsn, ssn = trunk_layer(xs, c_sample, pos_s, attn_s, ret_s,
                                        layer, lam, lam_init, *weights)
        kp_l.append(kp); vp_l.append(vp); sp_l.append(sp)
        ks_l.append(ksn); vs_l.append(vsn); ss_l.append(ssn)
    return (xp, xs, jnp.stack(kp_l), jnp.stack(vp_l), jnp.stack(sp_l),
            jnp.stack(ks_l), jnp.stack(vs_l), jnp.stack(ss_l))
```

```python
import functools
import math

import jax
import jax.numpy as jnp
from jax import lax
from jax.experimental import pallas as pl
from jax.experimental.pallas import tpu as pltpu

F32 = jnp.float32
BF16 = jnp.bfloat16
I32 = jnp.int32
SDS = jax.ShapeDtypeStruct

D_MODEL = 4096
H_DA, DK_DA, DV_DA = 8, 128, 256
H_RET, DK_RET, DV_RET = 8, 128, 256
W_DA = H_DA * DV_DA
W_RET = H_RET * DV_RET
OFF_DQ = 0
OFF_DK = OFF_DQ + H_DA * 2 * DK_DA
OFF_DV = OFF_DK + H_DA * 2 * DK_DA
OFF_RQ = OFF_DV + W_DA
OFF_RK = OFF_RQ + H_RET * DK_RET
OFF_RV = OFF_RK + H_RET * DK_RET
OFF_RG = OFF_RV + W_RET
D_IN = OFF_RG + W_RET
ROPE_THETA = 10000.0
RET_THETA = 10000.0
RET_CHUNK = 128
N_EXPERTS = 32
TOP_K = 4
D_FF = D_MODEL
SWIGLU_LIMIT = 7.0
SWIGLU_ALPHA = 1.702
NORM_EPS = 1e-6
NEG_INF = -1e30
N_ADA = 6
LANES = 128
MIB = 1024 * 1024

NT_DIMS = (((1,), (1,)), ((), ()))
TN_DIMS = (((0,), (0,)), ((), ()))


def _cparams(sem, vmem_mib):
    return pltpu.CompilerParams(dimension_semantics=sem, vmem_limit_bytes=vmem_mib * MIB)


def _rms(x):
    return x * lax.rsqrt(jnp.mean(x * x, axis=-1, keepdims=True) + NORM_EPS)


def _silu(x):
    return x * jax.nn.sigmoid(x)


def _ada_kernel(c_ref, w_ref, b_ref, o_ref):
    a = _silu(c_ref[...]).astype(BF16)
    o_ref[...] = jnp.dot(a, w_ref[...].astype(BF16), preferred_element_type=F32) + b_ref[...]


def _ada_mod(c_all, w_ada, b_ada):
    R, D = c_all.shape
    N = w_ada.shape[1]
    tn = 512
    return pl.pallas_call(
        _ada_kernel,
        out_shape=SDS((R, N), F32),
        grid=(N // tn,),
        in_specs=[pl.BlockSpec((R, D), lambda j: (0, 0)),
                  pl.BlockSpec((D, tn), lambda j: (0, j)),
                  pl.BlockSpec((1, tn), lambda j: (0, j))],
        out_specs=pl.BlockSpec((R, tn), lambda j: (0, j)),
        compiler_params=_cparams(("arbitrary",), 40),
        name="ada_mod",
    )(c_all, w_ada, b_ada.reshape(1, N))


def _mod_spec(mod, tm, col):
    if mod.shape[0] == 1:
        return pl.BlockSpec((1, D_MODEL), lambda i: (0, col))
    return pl.BlockSpec((tm, D_MODEL), lambda i: (i, col))


def _norm_mod_kernel(x_ref, g_ref, sc_ref, sh_ref, o_ref):
    h = (_rms(x_ref[...]) * g_ref[...]) * (1.0 + sc_ref[...]) + sh_ref[...]
    o_ref[...] = h.astype(o_ref.dtype)


def _norm_mod(x, g, mod, sh_col, sc_col, tm):
    M, D = x.shape
    return pl.pallas_call(
        _norm_mod_kernel,
        out_shape=SDS((M, D), BF16),
        grid=(M // tm,),
        in_specs=[pl.BlockSpec((tm, D), lambda i: (i, 0)),
                  pl.BlockSpec((1, D), lambda i: (0, 0)),
                  _mod_spec(mod, tm, sc_col), _mod_spec(mod, tm, sh_col)],
        out_specs=pl.BlockSpec((tm, D), lambda i: (i, 0)),
        compiler_params=_cparams(("arbitrary",), 48),
        name="norm_mod",
    )(x, g.reshape(1, D), mod, mod)


def _inproj_kernel(h_ref, w_ref, gq_ref, gk_ref, cf_ref, sf_ref, cr_ref, sr_ref, o_ref, wbf_ref, *, tn):
    j = pl.program_id(0)
    i = pl.program_id(1)

    @pl.when(i == 0)
    def _cast_weights():
        wbf_ref[...] = w_ref[...].astype(BF16)

    o_ref[...] = jnp.dot(h_ref[...], wbf_ref[...], preferred_element_type=F32)
    groups = [slice(c * LANES, (c + 1) * LANES) for c in range(tn // LANES)]

    @pl.when(j < OFF_DV // tn)
    def _qk_norm_rope():
        g = jnp.where(j < OFF_DK // tn, gq_ref[...], gk_ref[...])
        cf = cf_ref[...]
        sf = sf_ref[...]
        for sl in groups:
            y = _rms(o_ref[:, sl]) * g
            o_ref[:, sl] = y * cf + pltpu.roll(y, DK_DA // 2, 1) * sf

    @pl.when((j >= OFF_RQ // tn) & (j < OFF_RV // tn))
    def _ret_rotate():
        scale = jnp.where(j >= OFF_RK // tn, DK_RET ** -0.5, 1.0).astype(F32)
        cr = cr_ref[...]
        sr = sr_ref[...]
        even = (lax.broadcasted_iota(I32, cr.shape, 1) % 2) == 0
        for sl in groups:
            x = o_ref[:, sl]
            swapped = jnp.where(even, pltpu.roll(x, LANES - 1, 1), pltpu.roll(x, 1, 1))
            o_ref[:, sl] = (x * cr + swapped * sr) * scale


def _inproj(h, w_in, g_qn, g_kn, tabs, tm):
    M, D = h.shape
    tn = 512
    tab_spec = pl.BlockSpec((tm, LANES), lambda j, i: (i, 0))
    vec_spec = pl.BlockSpec((1, LANES), lambda j, i: (0, 0))
    return pl.pallas_call(
        functools.partial(_inproj_kernel, tn=tn),
        out_shape=SDS((M, D_IN), F32),
        grid=(D_IN // tn, M // tm),
        in_specs=[pl.BlockSpec((tm, D), lambda j, i: (i, 0)),
                  pl.BlockSpec((D, tn), lambda j, i: (0, j)),
                  vec_spec, vec_spec, tab_spec, tab_spec, tab_spec, tab_spec],
        out_specs=pl.BlockSpec((tm, tn), lambda j, i: (i, j)),
        scratch_shapes=[pltpu.VMEM((D, tn), BF16)],
        compiler_params=_cparams(("arbitrary", "arbitrary"), 56),
        name="inproj",
    )(h, w_in, g_qn.reshape(1, LANES), g_kn.reshape(1, LANES), *tabs)


def _rot_tables(pos):
    half = DK_DA // 2
    posf = pos.astype(F32)[:, None]
    inv = ROPE_THETA ** (-jnp.arange(half, dtype=F32) / half)
    ang = posf * inv[None, :]
    cos, sin = jnp.cos(ang), jnp.sin(ang)
    cf = jnp.concatenate([cos, cos], axis=-1)
    sf = jnp.concatenate([-sin, sin], axis=-1)
    inv_r = 1.0 / (RET_THETA ** jnp.linspace(0.0, 1.0, DK_RET // 2, dtype=F32))
    ang_r = posf * inv_r[None, :]
    cos_r, sin_r = jnp.cos(ang_r), jnp.sin(ang_r)
    cr = jnp.repeat(cos_r, 2, axis=-1)
    sr = jnp.stack([-sin_r, sin_r], axis=-1).reshape(pos.shape[0], DK_RET)
    return cf, sf, cr, sr


def _lam_value(lq_ref, lk_ref, lam_init):
    e = jnp.exp(jnp.sum(lq_ref[...] * lk_ref[...], axis=-1, keepdims=True))
    return e[0:1] - e[1:2] + lam_init


def _attn_kernel(lq_ref, lk_ref, gs_ref, q_ref, k_ref, v_ref, o_ref, qbf_ref, m_ref, l_ref, acc_ref,
                 *, tq, scale, lam_init):
    qi = pl.program_id(1)
    kj = pl.program_id(2)

    @pl.when(kj == 0)
    def _init():
        qbf_ref[...] = q_ref[...].astype(BF16)
        m_ref[...] = jnp.full(m_ref.shape, NEG_INF, F32)
        l_ref[...] = jnp.zeros(l_ref.shape, F32)
        acc_ref[...] = jnp.zeros(acc_ref.shape, F32)

    @pl.when(kj <= qi)
    def _step():
        kb = k_ref[...].astype(BF16)
        vb = v_ref[...].astype(BF16)
        row = qi * tq + lax.broadcasted_iota(I32, (tq, tq), 0)
        col = kj * tq + lax.broadcasted_iota(I32, (tq, tq), 1)
        mask = col <= row
        for mm in range(2):
            sl = slice(mm * DK_DA, (mm + 1) * DK_DA)
            s = lax.dot_general(qbf_ref[:, sl], kb[:, sl], NT_DIMS, preferred_element_type=F32) * scale
            s = jnp.where(mask, s, NEG_INF)
            m_prev = m_ref[mm]
            m_new = jnp.maximum(m_prev, jnp.max(s, axis=-1, keepdims=True))
            alpha = jnp.exp(m_prev - m_new)
            p = jnp.exp(s - m_new)
            l_ref[mm] = l_ref[mm] * alpha + jnp.sum(p, axis=-1, keepdims=True)
            acc_ref[mm] = acc_ref[mm] * alpha + jnp.dot(p.astype(BF16), vb, preferred_element_type=F32)
            m_ref[mm] = m_new

    @pl.when(kj == qi)
    def _finish():
        lam = _lam_value(lq_ref, lk_ref, lam_init)
        o = acc_ref[0] / l_ref[0] - lam * (acc_ref[1] / l_ref[1])
        o_ref[...] = (_rms(o) * gs_ref[...] * (1.0 - lam_init)).astype(o_ref.dtype)


def _attn_prompt(z, lam_q, lam_k, g_sub, lam_init):
    S = z.shape[0]
    tq = 512
    nb = S // tq
    wq = 2 * DK_DA
    return pl.pallas_call(
        functools.partial(_attn_kernel, tq=tq, scale=DK_DA ** -0.5, lam_init=lam_init),
        out_shape=SDS((S, W_DA), BF16),
        grid=(H_DA, nb, nb),
        in_specs=[pl.BlockSpec((2, DK_DA), lambda h, qi, kj: (0, 0)),
                  pl.BlockSpec((2, DK_DA), lambda h, qi, kj: (0, 0)),
                  pl.BlockSpec((1, DV_DA), lambda h, qi, kj: (0, 0)),
                  pl.BlockSpec((tq, wq), lambda h, qi, kj: (qi, OFF_DQ // wq + h)),
                  pl.BlockSpec((tq, wq), lambda h, qi, kj: (jnp.minimum(kj, qi), OFF_DK // wq + h)),
                  pl.BlockSpec((tq, DV_DA), lambda h, qi, kj: (jnp.minimum(kj, qi), OFF_DV // DV_DA + h))],
        out_specs=pl.BlockSpec((tq, DV_DA), lambda h, qi, kj: (qi, h)),
        scratch_shapes=[pltpu.VMEM((tq, wq), BF16),
                        pltpu.VMEM((2, tq, 1), F32), pltpu.VMEM((2, tq, 1), F32),
                        pltpu.VMEM((2, tq, DV_DA), F32)],
        compiler_params=_cparams(("arbitrary", "arbitrary", "arbitrary"), 32),
        name="attn_prompt",
    )(lam_q, lam_k, g_sub.reshape(1, DV_DA), z, z, z)


def _attn_s_kernel(pt_ref, lq_ref, lk_ref, gs_ref, q_ref, kn_ref, vn_ref, kc_ref, vc_ref, o_ref,
                   m_ref, l_ref, acc_ref, *, n_pages, scale, lam_init):
    p = pl.program_id(1)
    n_maps = 2 * H_DA
    q = q_ref[0]
    qb = q.astype(BF16)
    rowid = lax.broadcasted_iota(I32, (n_maps, 1), 0)

    @pl.when(p == 0)
    def _init_with_new_token():
        m_ref[...] = jnp.sum(q * kn_ref[0], axis=-1, keepdims=True) * scale
        l_ref[...] = jnp.ones(l_ref.shape, F32)
        vn = vn_ref[0]
        acc = jnp.zeros(acc_ref.shape, F32)
        for h in range(H_DA):
            acc = jnp.where(rowid % H_DA == h, vn[h:h + 1, :], acc)
        acc_ref[...] = acc

    s = jnp.zeros((n_maps, LANES), F32)
    for r in range(n_maps):
        col = ((r % H_DA) * 2 + r // H_DA) * DK_DA
        kr = kc_ref[0, :, col:col + DK_DA].astype(BF16)
        sr = lax.dot_general(qb, kr, NT_DIMS, preferred_element_type=F32)
        s = jnp.where(rowid == r, sr, s)
    s = s * scale
    m_prev = m_ref[...]
    m_new = jnp.maximum(m_prev, jnp.max(s, axis=-1, keepdims=True))
    alpha = jnp.exp(m_prev - m_new)
    pr = jnp.exp(s - m_new)
    l_ref[...] = l_ref[...] * alpha + jnp.sum(pr, axis=-1, keepdims=True)
    m_ref[...] = m_new
    pb = pr.astype(BF16)
    pv = jnp.zeros(acc_ref.shape, F32)
    for h in range(H_DA):
        vh = vc_ref[0, :, h * DV_DA:(h + 1) * DV_DA].astype(BF16)
        pv = jnp.where(rowid % H_DA == h, jnp.dot(pb, vh, preferred_element_type=F32), pv)
    acc_ref[...] = acc_ref[...] * alpha + pv

    @pl.when(p == n_pages - 1)
    def _finish():
        lam = _lam_value(lq_ref, lk_ref, lam_init)
        on = acc_ref[...] / l_ref[...]
        o = on[0:H_DA] - lam * on[H_DA:n_maps]
        o_ref[0] = _rms(o) * gs_ref[...] * (1.0 - lam_init)


def _attn_sample(q, k_new, v_new, cache_k, cache_v, page_table, lam_q, lam_k, g_sub, lam_init):
    B = q.shape[0]
    n_pages = page_table.shape[1]
    n_pool, page = cache_k.shape[0], cache_k.shape[1]
    n_maps = 2 * H_DA
    grid_spec = pltpu.PrefetchScalarGridSpec(
        num_scalar_prefetch=1,
        grid=(B, n_pages),
        in_specs=[pl.BlockSpec((2, DK_DA), lambda b, p, pt: (0, 0)),
                  pl.BlockSpec((2, DK_DA), lambda b, p, pt: (0, 0)),
                  pl.BlockSpec((1, DV_DA), lambda b, p, pt: (0, 0)),
                  pl.BlockSpec((1, n_maps, DK_DA), lambda b, p, pt: (b, 0, 0)),
                  pl.BlockSpec((1, n_maps, DK_DA), lambda b, p, pt: (b, 0, 0)),
                  pl.BlockSpec((1, H_DA, DV_DA), lambda b, p, pt: (b, 0, 0)),
                  pl.BlockSpec((1, page, n_maps * DK_DA), lambda b, p, pt: (pt[b * n_pages + p], 0, 0)),
                  pl.BlockSpec((1, page, W_DA), lambda b, p, pt: (pt[b * n_pages + p], 0, 0))],
        out_specs=pl.BlockSpec((1, H_DA, DV_DA), lambda b, p, pt: (b, 0, 0)),
        scratch_shapes=[pltpu.VMEM((n_maps, 1), F32), pltpu.VMEM((n_maps, 1), F32),
                        pltpu.VMEM((n_maps, DV_DA), F32)],
    )
    return pl.pallas_call(
        functools.partial(_attn_s_kernel, n_pages=n_pages, scale=DK_DA ** -0.5, lam_init=lam_init),
        out_shape=SDS((B, H_DA, DV_DA), F32),
        grid_spec=grid_spec,
        compiler_params=_cparams(("arbitrary", "arbitrary"), 32),
        name="attn_sample",
    )(page_table.reshape(-1), lam_q, lam_k, g_sub.reshape(1, DV_DA), q, k_new, v_new,
      cache_k.reshape(n_pool, page, n_maps * DK_DA), cache_v.reshape(n_pool, page, W_DA))


def _ret_decay_tables(chunk):
    lg = jnp.log1p(-jnp.exp2(-5.0 - jnp.arange(H_RET, dtype=F32)))
    idx = jnp.arange(chunk, dtype=F32)
    diff = idx[:, None] - idx[None, :]
    causal = diff >= 0
    dmask = jnp.where(causal, jnp.exp(jnp.where(causal, diff, 0.0)[None] * lg[:, None, None]), 0.0)
    qdec = jnp.exp((idx + 1.0)[None, :] * lg[:, None])[..., None]
    kdec = jnp.exp((chunk - 1.0 - idx)[None, :] * lg[:, None])[..., None]
    sdec = jnp.exp(chunk * lg)[:, None, None]
    return dmask, qdec, kdec, sdec


def _ret_prompt_kernel(dm_ref, qd_ref, kd_ref, sd_ref, gr_ref, q_ref, k_ref, v_ref, g_ref,
                       o_ref, st_ref, state_ref):
    c = pl.program_id(1)

    @pl.when(c == 0)
    def _zero_state():
        state_ref[...] = jnp.zeros(state_ref.shape, F32)

    q = q_ref[...].astype(BF16)
    k = k_ref[...]
    v = v_ref[...].astype(BF16)
    st = state_ref[...]
    qk = lax.dot_general(q, k.astype(BF16), NT_DIMS, preferred_element_type=F32) * dm_ref[0]
    inner = jnp.dot(qk.astype(BF16), v, preferred_element_type=F32)
    cross = jnp.dot(q, st.astype(BF16), preferred_element_type=F32) * qd_ref[0]
    kdecayed = (k * kd_ref[0]).astype(BF16)
    state_ref[...] = sd_ref[0] * st + lax.dot_general(kdecayed, v, TN_DIMS, preferred_element_type=F32)
    o = inner + cross
    o_ref[...] = (_rms(o) * gr_ref[...] * _silu(g_ref[...])).astype(o_ref.dtype)

    @pl.when(c == pl.num_programs(1) - 1)
    def _emit_state():
        st_ref[0] = state_ref[...]


def _ret_prompt(z, g_ret):
    S = z.shape[0]
    C = RET_CHUNK
    dmask, qdec, kdec, sdec = _ret_decay_tables(C)
    per_head = lambda shape: pl.BlockSpec((1,) + shape, lambda h, c: (h, 0, 0))
    return pl.pallas_call(
        _ret_prompt_kernel,
        out_shape=(SDS((S, W_RET), BF16), SDS((H_RET, DK_RET, DV_RET), F32)),
        grid=(H_RET, S // C),
        in_specs=[per_head((C, C)), per_head((C, 1)), per_head((C, 1)), per_head((1, 1)),
                  pl.BlockSpec((1, DV_RET), lambda h, c: (0, 0)),
                  pl.BlockSpec((C, DK_RET), lambda h, c: (c, OFF_RQ // DK_RET + h)),
                  pl.BlockSpec((C, DK_RET), lambda h, c: (c, OFF_RK // DK_RET + h)),
                  pl.BlockSpec((C, DV_RET), lambda h, c: (c, OFF_RV // DV_RET + h)),
                  pl.BlockSpec((C, DV_RET), lambda h, c: (c, OFF_RG // DV_RET + h))],
        out_specs=(pl.BlockSpec((C, DV_RET), lambda h, c: (c, h)),
                   pl.BlockSpec((1, DK_RET, DV_RET), lambda h, c: (h, 0, 0))),
        scratch_shapes=[pltpu.VMEM((DK_RET, DV_RET), F32)],
        compiler_params=_cparams(("arbitrary", "arbitrary"), 32),
        name="ret_prompt",
    )(dmask, qdec, kdec, sdec, g_ret.reshape(1, DV_RET), z, z, z, z)


def _ret_sample_kernel(dm_ref, qd_ref, kd_ref, sd_ref, gr_ref, q_ref, k_ref, v_ref, g_ref, s_ref,
                       o_ref, so_ref):
    q = q_ref[0]
    k = k_ref[0]
    v = v_ref[0]
    rowid = lax.broadcasted_iota(I32, (H_RET, 1), 0)
    qb = q.astype(BF16)
    vb = v.astype(BF16)
    kdecayed = k * kd_ref[...]
    sd = sd_ref[...]
    cross = jnp.zeros(v.shape, F32)
    for h in range(H_RET):
        st = s_ref[0, h]
        cross = jnp.where(rowid == h, jnp.dot(qb, st.astype(BF16), preferred_element_type=F32), cross)
        kh = jnp.where(rowid == h, kdecayed, 0.0).astype(BF16)
        so_ref[0, h] = sd[h:h + 1, :] * st + lax.dot_general(kh, vb, TN_DIMS, preferred_element_type=F32)
    qk = jnp.sum(q * k, axis=-1, keepdims=True) * dm_ref[...]
    o = qk * v + cross * qd_ref[...]
    o_ref[0] = _rms(o) * gr_ref[...] * _silu(g_ref[0])


def _ret_sample(q, k, v, g, state, g_ret):
    B = q.shape[0]
    dmask, qdec, kdec, sdec = _ret_decay_tables(1)
    head_vec = pl.BlockSpec((H_RET, 1), lambda b: (0, 0))
    row3 = lambda w: pl.BlockSpec((1, H_RET, w), lambda b: (b, 0, 0))
    st_spec = pl.BlockSpec((1, H_RET, DK_RET, DV_RET), lambda b: (b, 0, 0, 0))
    return pl.pallas_call(
        _ret_sample_kernel,
        out_shape=(SDS((B, H_RET, DV_RET), F32), SDS(state.shape, F32)),
        grid=(B,),
        in_specs=[head_vec, head_vec, head_vec, head_vec,
                  pl.BlockSpec((1, DV_RET), lambda b: (0, 0)),
                  row3(DK_RET), row3(DK_RET), row3(DV_RET), row3(DV_RET), st_spec],
        out_specs=(row3(DV_RET), st_spec),
        compiler_params=_cparams(("arbitrary",), 32),
        name="ret_sample",
    )(dmask.reshape(H_RET, 1), qdec.reshape(H_RET, 1), kdec.reshape(H_RET, 1), sdec.reshape(H_RET, 1),
      g_ret.reshape(1, DV_RET), q, k, v, g, state)


def _outproj_kernel(oa_ref, ob_ref, w_ref, x_ref, gt_ref, o_ref, wbf_ref):
    @pl.when(pl.program_id(1) == 0)
    def _cast_weights():
        wbf_ref[...] = w_ref[...].astype(BF16)

    acc = jnp.dot(oa_ref[...].astype(BF16), wbf_ref[0:W_DA, :], preferred_element_type=F32)
    acc += jnp.dot(ob_ref[...].astype(BF16), wbf_ref[W_DA:W_DA + W_RET, :], preferred_element_type=F32)
    o_ref[...] = x_ref[...] + gt_ref[...] * acc


def _outproj(o_da, o_ret, w_out, x, mod, gt_col, tm):
    M, D = x.shape
    tn = 512
    gt_spec = (pl.BlockSpec((1, tn), lambda j, i: (0, gt_col * (D // tn) + j)) if mod.shape[0] == 1
               else pl.BlockSpec((tm, tn), lambda j, i: (i, gt_col * (D // tn) + j)))
    return pl.pallas_call(
        _outproj_kernel,
        out_shape=SDS((M, D), F32),
        grid=(D // tn, M // tm),
        in_specs=[pl.BlockSpec((tm, W_DA), lambda j, i: (i, 0)),
                  pl.BlockSpec((tm, W_RET), lambda j, i: (i, 0)),
                  pl.BlockSpec((W_DA + W_RET, tn), lambda j, i: (0, j)),
                  pl.BlockSpec((tm, tn), lambda j, i: (i, j)),
                  gt_spec],
        out_specs=pl.BlockSpec((tm, tn), lambda j, i: (i, j)),
        scratch_shapes=[pltpu.VMEM((W_DA + W_RET, tn), BF16)],
        compiler_params=_cparams(("arbitrary", "arbitrary"), 56),
        name="outproj",
    )(o_da, o_ret, w_out, x, mod)


def _router_kernel(x_ref, g_ref, sc_ref, sh_ref, wr_ref, br_ref, h_ref, ti_ref, tg_ref):
    h = (_rms(x_ref[...]) * g_ref[...]) * (1.0 + sc_ref[...]) + sh_ref[...]
    h_ref[...] = h
    hb = h.astype(BF16)
    hl = (h - hb.astype(F32)).astype(BF16)
    w = wr_ref[...]
    wb = w.astype(BF16)
    wl = (w - wb.astype(F32)).astype(BF16)
    logits = (jnp.dot(hb, wb, preferred_element_type=F32) + jnp.dot(hb, wl, preferred_element_type=F32)
              + jnp.dot(hl, wb, preferred_element_type=F32)) + br_ref[...]
    lane = lax.broadcasted_iota(I32, logits.shape, 1)
    vals, idxs = [], []
    for _ in range(TOP_K):
        m = jnp.max(logits, axis=-1, keepdims=True)
        idx = jnp.min(jnp.where(logits == m, lane, LANES), axis=-1, keepdims=True)
        vals.append(m)
        idxs.append(idx)
        logits = jnp.where(lane == idx, -jnp.inf, logits)
    exps = [jnp.exp(v - vals[0]) for v in vals]
    denom = exps[0] + exps[1] + exps[2] + exps[3]
    ti = jnp.zeros(lane.shape, I32)
    tg = jnp.zeros(lane.shape, F32)
    for kk in range(TOP_K):
        ti = jnp.where(lane == kk, idxs[kk], ti)
        tg = jnp.where(lane == kk, exps[kk] / denom, tg)
    ti_ref[...] = ti
    tg_ref[...] = tg


def _norm_router(x, g, mod, sh_col, sc_col, wr_pad, br_pad, tm):
    M, D = x.shape
    return pl.pallas_call(
        _router_kernel,
        out_shape=(SDS((M, D), F32), SDS((M, LANES), I32), SDS((M, LANES), F32)),
        grid=(M // tm,),
        in_specs=[pl.BlockSpec((tm, D), lambda i: (i, 0)),
                  pl.BlockSpec((1, D), lambda i: (0, 0)),
                  _mod_spec(mod, tm, sc_col), _mod_spec(mod, tm, sh_col),
                  pl.BlockSpec((D, LANES), lambda i: (0, 0)),
                  pl.BlockSpec((1, LANES), lambda i: (0, 0))],
        out_specs=(pl.BlockSpec((tm, D), lambda i: (i, 0)),
                   pl.BlockSpec((tm, LANES), lambda i: (i, 0)),
                   pl.BlockSpec((tm, LANES), lambda i: (i, 0))),
        compiler_params=_cparams(("arbitrary",), 48),
        name="norm_router",
    )(x, g.reshape(1, D), mod, mod, wr_pad, br_pad)


def _route_tables(top_i, tm):
    T = top_i.shape[0]
    A = T * TOP_K
    flat_e = top_i.reshape(A)
    onehot = (flat_e[:, None] == jnp.arange(N_EXPERTS, dtype=I32)[None, :]).astype(I32)
    csum = jnp.cumsum(onehot, axis=0)
    rank = jnp.take_along_axis(csum, flat_e[:, None], axis=1)[:, 0] - 1
    counts = csum[-1]
    pcounts = (counts + tm - 1) // tm * tm
    pend = jnp.cumsum(pcounts)
    dest = (pend - pcounts)[flat_e] + rank
    n_blocks = (A + N_EXPERTS * (tm - 1) + tm - 1) // tm
    row_tok = jnp.zeros((n_blocks * tm,), I32).at[dest].set(jnp.arange(A, dtype=I32) // TOP_K)
    n_used = (pend[-1] // tm).astype(I32)
    blk = jnp.arange(n_blocks, dtype=I32)
    blk_e = jnp.minimum(jnp.searchsorted(pend, blk * tm, side='right'), N_EXPERTS - 1).astype(I32)
    blk_e = jnp.where(blk < n_used, blk_e, blk_e[jnp.maximum(n_used - 1, 0)])
    return dest.astype(I32), row_tok, blk_e, n_used.reshape(1)


def _gather_kernel(tok_ref, h_hbm, o_ref, buf_ref, sem, *, rows):
    def row_copy(r, src_row):
        return pltpu.make_async_copy(h_hbm.at[pl.ds(src_row, 1), :], buf_ref.at[pl.ds(r, 1), :], sem)

    def start(r, carry):
        row_copy(r, tok_ref[r]).start()
        return carry

    def wait(r, carry):
        row_copy(r, 0).wait()
        return carry

    lax.fori_loop(0, rows, start, 0)
    lax.fori_loop(0, rows, wait, 0)
    o_ref[...] = buf_ref[...].astype(o_ref.dtype)


def _gather_rows(h, row_tok, rows):
    P = row_tok.shape[0]
    D = h.shape[1]
    return pl.pallas_call(
        functools.partial(_gather_kernel, rows=rows),
        out_shape=SDS((P, D), BF16),
        grid=(P // rows,),
        in_specs=[pl.BlockSpec((rows,), lambda i: (i,), memory_space=pltpu.SMEM),
                  pl.BlockSpec(memory_space=pl.ANY)],
        out_specs=pl.BlockSpec((rows, D), lambda i: (i, 0)),
        scratch_shapes=[pltpu.VMEM((rows, D), F32), pltpu.SemaphoreType.DMA(())],
        compiler_params=_cparams(("arbitrary",), 32),
        name="moe_gather",
    )(row_tok, h)


def _new_expert(be_ref, i):
    return (i == 0) | (be_ref[i] != be_ref[jnp.maximum(i - 1, 0)])


def _moe_gu_kernel(be_ref, nu_ref, x_ref, w_ref, b_ref, o_ref, wbf_ref, *, tn):
    i = pl.program_id(1)

    @pl.when(_new_expert(be_ref, i))
    def _cast_weights():
        wbf_ref[...] = w_ref[0].astype(BF16)

    @pl.when(i < nu_ref[0])
    def _compute():
        gu = jnp.dot(x_ref[...], wbf_ref[...], preferred_element_type=F32) + b_ref[0]
        even = (lax.broadcasted_iota(I32, (x_ref.shape[0], LANES), 1) % 2) == 0
        for c in range(tn // (2 * LANES)):
            a = gu[:, (2 * c) * LANES:(2 * c + 1) * LANES]
            b = gu[:, (2 * c + 1) * LANES:(2 * c + 2) * LANES]
            gate = jnp.where(even, a, pltpu.roll(b, 1, 1))
            up = jnp.where(even, pltpu.roll(a, LANES - 1, 1), b)
            gate = jnp.minimum(gate, SWIGLU_LIMIT)
            up = jnp.clip(up, -SWIGLU_LIMIT, SWIGLU_LIMIT)
            act = (up + 1.0) * gate * jax.nn.sigmoid(SWIGLU_ALPHA * gate)
            o_ref[:, c * LANES:(c + 1) * LANES] = act.astype(o_ref.dtype)

    @pl.when(i >= nu_ref[0])
    def _unused_block():
        o_ref[...] = jnp.zeros(o_ref.shape, o_ref.dtype)


def _moe_gate_up(xs, w_gu, b_gu, blk_e, n_used, tm):
    P, D = xs.shape
    F2 = w_gu.shape[2]
    tn = 512
    grid_spec = pltpu.PrefetchScalarGridSpec(
        num_scalar_prefetch=2,
        grid=(F2 // tn, P // tm),
        in_specs=[pl.BlockSpec((tm, D), lambda j, i, be, nu: (i, 0)),
                  pl.BlockSpec((1, D, tn), lambda j, i, be, nu: (be[i], 0, j)),
                  pl.BlockSpec((1, 1, tn), lambda j, i, be, nu: (be[i], 0, j))],
        out_specs=pl.BlockSpec((tm, tn // 2), lambda j, i, be, nu: (i, j)),
        scratch_shapes=[pltpu.VMEM((D, tn), BF16)],
    )
    return pl.pallas_call(
        functools.partial(_moe_gu_kernel, tn=tn),
        out_shape=SDS((P, F2 // 2), BF16),
        grid_spec=grid_spec,
        compiler_params=_cparams(("arbitrary", "arbitrary"), 48),
        name="moe_gate_up",
    )(blk_e, n_used, xs, w_gu, b_gu.reshape(N_EXPERTS, 1, F2))


def _moe_down_kernel(be_ref, nu_ref, a_ref, w_ref, b_ref, o_ref, slab_ref, wbf_ref, *, tn):
    i = pl.program_id(1)
    half = LANES // 2

    @pl.when(_new_expert(be_ref, i))
    def _cast_and_interleave_weights():
        for c in range(tn // LANES):
            cols = slice(c * LANES, (c + 1) * LANES)
            for g in range(w_ref.shape[1] // LANES):
                slab_ref[c, pl.ds(g * LANES, half, stride=2), :] = w_ref[0, g * LANES:g * LANES + half, cols]
                slab_ref[c, pl.ds(g * LANES + 1, half, stride=2), :] = (
                    w_ref[0, g * LANES + half:(g + 1) * LANES, cols])
            wbf_ref[:, cols] = slab_ref[c].astype(BF16)

    @pl.when(i < nu_ref[0])
    def _compute():
        o_ref[...] = jnp.dot(a_ref[...], wbf_ref[...], preferred_element_type=F32) + b_ref[0]

    @pl.when(i >= nu_ref[0])
    def _unused_block():
        o_ref[...] = jnp.zeros(o_ref.shape, o_ref.dtype)


def _moe_down(act, w_down, b_down, blk_e, n_used, tm):
    P, F = act.shape
    D = w_down.shape[2]
    tn = 512
    grid_spec = pltpu.PrefetchScalarGridSpec(
        num_scalar_prefetch=2,
        grid=(D // tn, P // tm),
        in_specs=[pl.BlockSpec((tm, F), lambda j, i, be, nu: (i, 0)),
                  pl.BlockSpec((1, F, tn), lambda j, i, be, nu: (be[i], 0, j)),
                  pl.BlockSpec((1, 1, tn), lambda j, i, be, nu: (be[i], 0, j))],
        out_specs=pl.BlockSpec((tm, tn), lambda j, i, be, nu: (i, j)),
        scratch_shapes=[pltpu.VMEM((tn // LANES, F, LANES), F32), pltpu.VMEM((F, tn), BF16)],
    )
    return pl.pallas_call(
        functools.partial(_moe_down_kernel, tn=tn),
        out_shape=SDS((P, D), F32),
        grid_spec=grid_spec,
        compiler_params=_cparams(("arbitrary", "arbitrary"), 56),
        name="moe_down",
    )(blk_e, n_used, act, w_down, b_down.reshape(N_EXPERTS, 1, D))


def _combine_kernel(dest_ref, y_hbm, tg_ref, x_ref, gt_ref, o_ref, buf_ref, sem, *, tb):
    n = TOP_K * tb

    def row_copy(slot, src_row):
        return pltpu.make_async_copy(y_hbm.at[pl.ds(src_row, 1), :], buf_ref.at[pl.ds(slot, 1), :], sem)

    def start(r, carry):
        row_copy((r % TOP_K) * tb + r // TOP_K, dest_ref[r]).start()
        return carry

    def wait(r, carry):
        row_copy(r, 0).wait()
        return carry

    lax.fori_loop(0, n, start, 0)
    lax.fori_loop(0, n, wait, 0)
    tg = tg_ref[...]
    y = buf_ref[0:tb, :] * tg[:, 0:1]
    for kk in range(1, TOP_K):
        y += buf_ref[kk * tb:(kk + 1) * tb, :] * tg[:, kk:kk + 1]
    o_ref[...] = x_ref[...] + gt_ref[...] * y


def _combine(yb, dest, gates, x, mod, gt_col, tb):
    M, D = x.shape
    return pl.pallas_call(
        functools.partial(_combine_kernel, tb=tb),
        out_shape=SDS((M, D), F32),
        grid=(M // tb,),
        in_specs=[pl.BlockSpec((TOP_K * tb,), lambda i: (i,), memory_space=pltpu.SMEM),
                  pl.BlockSpec(memory_space=pl.ANY),
                  pl.BlockSpec((tb, LANES), lambda i: (i, 0)),
                  pl.BlockSpec((tb, D), lambda i: (i, 0)),
                  _mod_spec(mod, tb, gt_col)],
        out_specs=pl.BlockSpec((tb, D), lambda i: (i, 0)),
        scratch_shapes=[pltpu.VMEM((TOP_K * tb, D), F32), pltpu.SemaphoreType.DMA(())],
        compiler_params=_cparams(("arbitrary",), 40),
        name="moe_combine",
    )(dest, yb, gates, x, mod)


def kernel(x_prompt, x_sample, cache_k, cache_v, state_ret, page_table, c_prompt, c_sample,
           w_ada, b_ada, g_attn, w_in, g_qn, g_kn, lam_q, lam_k, g_sub, g_ret, w_out,
           g_ffn, w_router, b_router, w_gu, b_gu, w_down, b_down):
    depth = w_ada.shape[0]
    assert depth == 1 and x_prompt.shape[0] == 1 and x_sample.shape[1] == 1
    layer = 0
    S = x_prompt.shape[1]
    Bd = x_sample.shape[0]
    D = D_MODEL
    lam_init = 0.8 - 0.6 * math.exp(-0.3 * layer)
    past = page_table.shape[1] * cache_k.shape[2]
    tm_p = 1024
    tm_moe = 256

    xp = x_prompt.reshape(S, D)
    xs = x_sample.reshape(Bd, D)

    n_c = 1 + Bd
    n_c_pad = -(-n_c // 16) * 16
    c_all = jnp.concatenate([c_prompt, c_sample, jnp.zeros((n_c_pad - n_c, D), F32)], axis=0)
    mod = _ada_mod(c_all, w_ada[layer], b_ada[layer])
    mod_p, mod_s = mod[0:1], mod[1:n_c]
    SH1, SC1, GT1, SH2, SC2, GT2 = range(N_ADA)

    tabs_p = _rot_tables(jnp.arange(S))
    tabs_s = _rot_tables(jnp.full((Bd,), past, I32))

    h_p = _norm_mod(xp, g_attn[layer], mod_p, SH1, SC1, 512)
    z_p = _inproj(h_p, w_in[layer], g_qn[layer], g_kn[layer], tabs_p, tm_p)
    o_da_p = _attn_prompt(z_p, lam_q[layer], lam_k[layer], g_sub[layer], lam_init)
    o_ret_p, st_p = _ret_prompt(z_p, g_ret[layer])
    x1_p = _outproj(o_da_p, o_ret_p, w_out[layer], xp, mod_p, GT1, tm_p)

    h_s = _norm_mod(xs, g_attn[layer], mod_s, SH1, SC1, Bd)
    z_s = _inproj(h_s, w_in[layer], g_qn[layer], g_kn[layer], tabs_s, Bd)
    n_maps = 2 * H_DA
    by_map = lambda a: a.reshape(Bd, H_DA, 2, DK_DA).swapaxes(1, 2).reshape(Bd, n_maps, DK_DA)
    o_da_s = _attn_sample(
        by_map(z_s[:, OFF_DQ:OFF_DK]), by_map(z_s[:, OFF_DK:OFF_DV]),
        z_s[:, OFF_DV:OFF_RQ].reshape(Bd, H_DA, DV_DA), cache_k[layer], cache_v[layer], page_table,
        lam_q[layer], lam_k[layer], g_sub[layer], lam_init)
    o_ret_s, st_s = _ret_sample(
        z_s[:, OFF_RQ:OFF_RK].reshape(Bd, H_RET, DK_RET), z_s[:, OFF_RK:OFF_RV].reshape(Bd, H_RET, DK_RET),
        z_s[:, OFF_RV:OFF_RG].reshape(Bd, H_RET, DV_RET), z_s[:, OFF_RG:D_IN].reshape(Bd, H_RET, DV_RET),
        state_ret[layer], g_ret[layer])
    x1_s = _outproj(o_da_s.reshape(Bd, W_DA), o_ret_s.reshape(Bd, W_RET), w_out[layer], xs, mod_s, GT1, Bd)

    wr_pad = jnp.pad(w_router[layer], ((0, 0), (0, LANES - N_EXPERTS)))
    br_pad = jnp.pad(b_router[layer], (0, LANES - N_EXPERTS), constant_values=NEG_INF).reshape(1, LANES)
    h2_p, ti_p, tg_p = _norm_router(x1_p, g_ffn[layer], mod_p, SH2, SC2, wr_pad, br_pad, 512)
    h2_s, ti_s, tg_s = _norm_router(x1_s, g_ffn[layer], mod_s, SH2, SC2, wr_pad, br_pad, Bd)
    h2 = jnp.concatenate([h2_p, h2_s], axis=0)
    top_i = jnp.concatenate([ti_p[:, :TOP_K], ti_s[:, :TOP_K]], axis=0)
    dest, row_tok, blk_e, n_used = _route_tables(top_i, tm_moe)
    xg = _gather_rows(h2, row_tok, tm_moe)
    act = _moe_gate_up(xg, w_gu[layer], b_gu[layer], blk_e, n_used, tm_moe)
    yb = _moe_down(act, w_down[layer], b_down[layer], blk_e, n_used, tm_moe)
    y_p = _combine(yb, dest[:S * TOP_K], tg_p, x1_p, mod_p, GT2, 128)
    y_s = _combine(yb, dest[S * TOP_K:], tg_s, x1_s, mod_s, GT2, Bd)

    return (y_p.reshape(1, S, D), y_s.reshape(Bd, 1, D),
            z_p[:, OFF_DK:OFF_DV].reshape(1, 1, S, H_DA, 2, DK_DA),
            z_p[:, OFF_DV:OFF_RQ].reshape(1, 1, S, H_DA, DV_DA),
            st_p.reshape(1, 1, H_RET, DK_RET, DV_RET),
            z_s[:, OFF_DK:OFF_DV].reshape(1, Bd, 1, H_DA, 2, DK_DA),
            z_s[:, OFF_DV:OFF_RQ].reshape(1, Bd, 1, H_DA, DV_DA),
            st_s.reshape(1, Bd, H_RET, DK_RET, DV_RET))
```

```python
import functools
import math

import numpy as np
import jax
import jax.numpy as jnp
from jax import lax
from jax.experimental import pallas as pl
from jax.experimental.pallas import tpu as pltpu

F32 = jnp.float32
BF16 = jnp.bfloat16
I32 = jnp.int32
SDS = jax.ShapeDtypeStruct

D_MODEL = 4096
H_DA, DK_DA, DV_DA = 8, 128, 256
H_RET, DK_RET, DV_RET = 8, 128, 256
W_DA = H_DA * DV_DA
W_RET = H_RET * DV_RET
OFF_DQ = 0
OFF_DK = OFF_DQ + H_DA * 2 * DK_DA
OFF_DV = OFF_DK + H_DA * 2 * DK_DA
OFF_RQ = OFF_DV + W_DA
OFF_RK = OFF_RQ + H_RET * DK_RET
OFF_RV = OFF_RK + H_RET * DK_RET
OFF_RG = OFF_RV + W_RET
D_IN = OFF_RG + W_RET
ROPE_THETA = 10000.0
RET_THETA = 10000.0
RET_CHUNK = 128
N_EXPERTS = 32
TOP_K = 4
SWIGLU_LIMIT = 7.0
SWIGLU_ALPHA = 1.702
NORM_EPS = 1e-6
NEG_INF = -1e30
N_ADA = 6
LOG2E = 1.4426950408889634
LANES = 128
MXU_N = 256
MIB = 1024 * 1024

NT_DIMS = (((1,), (1,)), ((), ()))
TN_DIMS = (((0,), (0,)), ((), ()))


def _cparams(sem, vmem_mib):
    return pltpu.CompilerParams(dimension_semantics=sem, vmem_limit_bytes=vmem_mib * MIB)


def _rms(x):
    return x * lax.rsqrt(jnp.mean(x * x, axis=-1, keepdims=True) + NORM_EPS)


def _silu(x):
    return x * jax.nn.sigmoid(x)


def _lane_tile(x, width):
    return jnp.concatenate([x] * (width // LANES), axis=1)


def _ada_kernel(c_ref, w_ref, b_ref, o_ref):
    a = _silu(c_ref[...]).astype(BF16)
    o_ref[...] = jnp.dot(a, w_ref[...].astype(BF16), preferred_element_type=F32) + b_ref[...]


def _ada_mod(c_all, w_ada, b_ada, layer):
    R, D = c_all.shape
    N = w_ada.shape[2]
    tn = 512
    return pl.pallas_call(
        _ada_kernel,
        out_shape=SDS((R, N), F32),
        grid=(N // tn,),
        in_specs=[pl.BlockSpec((R, D), lambda j: (0, 0)),
                  pl.BlockSpec((None, D, tn), lambda j: (layer, 0, j)),
                  pl.BlockSpec((1, tn), lambda j: (layer, j))],
        out_specs=pl.BlockSpec((R, tn), lambda j: (0, j)),
        compiler_params=_cparams(("arbitrary",), 40),
        name="ada_mod",
    )(c_all, w_ada, b_ada)


def _mod_spec(mod, tm, col):
    if mod.shape[0] == 1:
        return pl.BlockSpec((1, D_MODEL), lambda i: (0, col))
    return pl.BlockSpec((tm, D_MODEL), lambda i: (i, col))


def _norm_mod_kernel(x_ref, g_ref, sc_ref, sh_ref, o_ref):
    h = (_rms(x_ref[...]) * g_ref[...]) * (1.0 + sc_ref[...]) + sh_ref[...]
    o_ref[...] = h.astype(o_ref.dtype)


def _norm_mod(x, g, mod, sh_col, sc_col, tm):
    M, D = x.shape
    return pl.pallas_call(
        _norm_mod_kernel,
        out_shape=SDS((M, D), BF16),
        grid=(M // tm,),
        in_specs=[pl.BlockSpec((tm, D), lambda i: (i, 0)),
                  pl.BlockSpec((1, D), lambda i: (0, 0)),
                  _mod_spec(mod, tm, sc_col), _mod_spec(mod, tm, sh_col)],
        out_specs=pl.BlockSpec((tm, D), lambda i: (i, 0)),
        compiler_params=_cparams(("arbitrary",), 48),
        name="norm_mod",
    )(x, g.reshape(1, D), mod, mod)


def _inproj_kernel(h_ref, w_ref, gq_ref, gk_ref, cf_ref, sf_ref, cr_ref, sr_ref, o_ref, wbf_ref, *, tn):
    j = pl.program_id(0)
    i = pl.program_id(1)

    @pl.when(i == 0)
    def _cast_weights():
        wbf_ref[...] = w_ref[...].astype(BF16)

    o_ref[...] = jnp.dot(h_ref[...], wbf_ref[...], preferred_element_type=F32)
    groups = [slice(c * LANES, (c + 1) * LANES) for c in range(tn // LANES)]

    @pl.when(j < OFF_DV // tn)
    def _qk_norm_rope():
        g = jnp.where(j < OFF_DK // tn, gq_ref[...], gk_ref[...])
        cf = cf_ref[...]
        sf = sf_ref[...]
        for sl in groups:
            y = _rms(o_ref[:, sl]) * g
            o_ref[:, sl] = y * cf + pltpu.roll(y, DK_DA // 2, 1) * sf

    @pl.when((j >= OFF_RQ // tn) & (j < OFF_RV // tn))
    def _ret_rotate():
        scale = jnp.where(j >= OFF_RK // tn, DK_RET ** -0.5, 1.0).astype(F32)
        cr = cr_ref[...]
        sr = sr_ref[...]
        even = (lax.broadcasted_iota(I32, cr.shape, 1) % 2) == 0
        for sl in groups:
            x = o_ref[:, sl]
            swapped = jnp.where(even, pltpu.roll(x, LANES - 1, 1), pltpu.roll(x, 1, 1))
            o_ref[:, sl] = (x * cr + swapped * sr) * scale


def _inproj(h, w_in, layer, g_qn, g_kn, tabs, tm):
    M, D = h.shape
    tn = 512
    tab_spec = pl.BlockSpec((tm, LANES), lambda j, i: (i, 0))
    vec_spec = pl.BlockSpec((1, LANES), lambda j, i: (0, 0))
    return pl.pallas_call(
        functools.partial(_inproj_kernel, tn=tn),
        out_shape=SDS((M, D_IN), F32),
        grid=(D_IN // tn, M // tm),
        in_specs=[pl.BlockSpec((tm, D), lambda j, i: (i, 0)),
                  pl.BlockSpec((None, D, tn), lambda j, i: (layer, 0, j)),
                  vec_spec, vec_spec, tab_spec, tab_spec, tab_spec, tab_spec],
        out_specs=pl.BlockSpec((tm, tn), lambda j, i: (i, j)),
        scratch_shapes=[pltpu.VMEM((D, tn), BF16)],
        compiler_params=_cparams(("arbitrary", "arbitrary"), 56),
        name="inproj",
    )(h, w_in, g_qn.reshape(1, LANES), g_kn.reshape(1, LANES), *tabs)


def _rot_tables(pos):
    half = DK_DA // 2
    posf = pos.astype(F32)[:, None]
    inv = ROPE_THETA ** (-jnp.arange(half, dtype=F32) / half)
    ang = posf * inv[None, :]
    cos, sin = jnp.cos(ang), jnp.sin(ang)
    cf = jnp.concatenate([cos, cos], axis=-1)
    sf = jnp.concatenate([-sin, sin], axis=-1)
    inv_r = 1.0 / (RET_THETA ** jnp.linspace(0.0, 1.0, DK_RET // 2, dtype=F32))
    ang_r = posf * inv_r[None, :]
    cos_r, sin_r = jnp.cos(ang_r), jnp.sin(ang_r)
    cr = jnp.repeat(cos_r, 2, axis=-1)
    sr = jnp.stack([-sin_r, sin_r], axis=-1).reshape(pos.shape[0], DK_RET)
    return cf, sf, cr, sr


def _lam_value(lq_ref, lk_ref, lam_init):
    e = jnp.exp(jnp.sum(lq_ref[...] * lk_ref[...], axis=-1, keepdims=True))
    return e[0:1] - e[1:2] + lam_init


def _attn_kernel(qi_ref, kj_ref, lq_ref, lk_ref, gs_ref, q_ref, k_ref, v_ref, o_ref,
                 qbf_ref, m_ref, l_ref, acc_ref, *, tq, qscale, lam_init):
    t = pl.program_id(1)
    qi = qi_ref[t]
    kj = kj_ref[t]

    @pl.when(kj == 0)
    def _init():
        qbf_ref[...] = (q_ref[...] * qscale).astype(BF16)
        m_ref[...] = jnp.full(m_ref.shape, NEG_INF, F32)
        l_ref[...] = jnp.zeros(l_ref.shape, F32)
        acc_ref[...] = jnp.zeros(acc_ref.shape, F32)

    def step(on_diagonal):
        kb = k_ref[...].astype(BF16)
        vb = v_ref[...].astype(BF16)
        if on_diagonal:
            keep = lax.broadcasted_iota(I32, (tq, tq), 1) <= lax.broadcasted_iota(I32, (tq, tq), 0)
        for mm in range(2):
            sl = slice(mm * DK_DA, (mm + 1) * DK_DA)
            s = lax.dot_general(qbf_ref[:, sl], kb[:, sl], NT_DIMS, preferred_element_type=F32)
            if on_diagonal:
                s = jnp.where(keep, s, NEG_INF)
            m_prev = m_ref[mm]
            m_new = jnp.maximum(m_prev, jnp.max(s, axis=-1, keepdims=True))
            alpha = jnp.exp2(m_prev - m_new)
            p = jnp.exp2(s - _lane_tile(m_new, tq))
            l_ref[mm] = l_ref[mm] * alpha + jnp.sum(p, axis=-1, keepdims=True)
            acc_ref[mm] = (acc_ref[mm] * _lane_tile(alpha, DV_DA)
                           + jnp.dot(p.astype(BF16), vb, preferred_element_type=F32))
            m_ref[mm] = m_new

    @pl.when(kj < qi)
    def _below_diagonal():
        step(False)

    @pl.when(kj == qi)
    def _diagonal_and_finish():
        step(True)
        lam = _lam_value(lq_ref, lk_ref, lam_init)
        o = (acc_ref[0] / _lane_tile(l_ref[0], DV_DA)
             - lam * (acc_ref[1] / _lane_tile(l_ref[1], DV_DA)))
        o_ref[...] = (_rms(o) * gs_ref[...] * (1.0 - lam_init)).astype(o_ref.dtype)


def _attn_prompt(z, lam_q, lam_k, g_sub, lam_init):
    S = z.shape[0]
    tq = 512
    nb = S // tq
    wq = 2 * DK_DA
    pairs = [(qi, kj) for qi in range(nb) for kj in range(qi + 1)]
    qi_tab = jnp.asarray(np.array([p[0] for p in pairs], np.int32))
    kj_tab = jnp.asarray(np.array([p[1] for p in pairs], np.int32))
    grid_spec = pltpu.PrefetchScalarGridSpec(
        num_scalar_prefetch=2,
        grid=(H_DA, len(pairs)),
        in_specs=[pl.BlockSpec((2, DK_DA), lambda h, t, qt, kt: (0, 0)),
                  pl.BlockSpec((2, DK_DA), lambda h, t, qt, kt: (0, 0)),
                  pl.BlockSpec((1, DV_DA), lambda h, t, qt, kt: (0, 0)),
                  pl.BlockSpec((tq, wq), lambda h, t, qt, kt: (qt[t], OFF_DQ // wq + h)),
                  pl.BlockSpec((tq, wq), lambda h, t, qt, kt: (kt[t], OFF_DK // wq + h)),
                  pl.BlockSpec((tq, DV_DA), lambda h, t, qt, kt: (kt[t], OFF_DV // DV_DA + h))],
        out_specs=pl.BlockSpec((tq, DV_DA), lambda h, t, qt, kt: (qt[t], h)),
        scratch_shapes=[pltpu.VMEM((tq, wq), BF16),
                        pltpu.VMEM((2, tq, LANES), F32), pltpu.VMEM((2, tq, LANES), F32),
                        pltpu.VMEM((2, tq, DV_DA), F32)],
    )
    return pl.pallas_call(
        functools.partial(_attn_kernel, tq=tq, qscale=DK_DA ** -0.5 * LOG2E, lam_init=lam_init),
        out_shape=SDS((S, W_DA), BF16),
        grid_spec=grid_spec,
        compiler_params=_cparams(("arbitrary", "arbitrary"), 32),
        name="attn_prompt",
    )(qi_tab, kj_tab, lam_q, lam_k, g_sub.reshape(1, DV_DA), z, z, z)


def _attn_s_kernel(pt_ref, lq_ref, lk_ref, gs_ref, q_ref, kn_ref, vn_ref, *rest, pg, scale, lam_init):
    kc_refs = rest[:pg]
    vc_refs = rest[pg:2 * pg]
    o_ref, m_ref, l_ref, acc_ref = rest[2 * pg:]
    step = pl.program_id(1)
    n_maps = 2 * H_DA
    q = q_ref[0]
    qb = q.astype(BF16)
    rowid = lax.broadcasted_iota(I32, (n_maps, 1), 0)

    @pl.when(step == 0)
    def _init_with_new_token():
        m_ref[...] = jnp.sum(q * kn_ref[0], axis=-1, keepdims=True) * scale
        l_ref[...] = jnp.ones(l_ref.shape, F32)
        vn = vn_ref[0]
        acc = jnp.zeros(acc_ref.shape, F32)
        for h in range(H_DA):
            acc = jnp.where(rowid % H_DA == h, vn[h:h + 1, :], acc)
        acc_ref[...] = acc

    parts = []
    for kc_ref in kc_refs:
        s = jnp.zeros((n_maps, LANES), F32)
        for r in range(n_maps):
            col = ((r % H_DA) * 2 + r // H_DA) * DK_DA
            kr = kc_ref[0, :, col:col + DK_DA].astype(BF16)
            s = jnp.where(rowid == r, lax.dot_general(qb, kr, NT_DIMS, preferred_element_type=F32), s)
        parts.append(s)
    s = jnp.concatenate(parts, axis=1) * scale
    m_prev = m_ref[...]
    m_new = jnp.maximum(m_prev, jnp.max(s, axis=-1, keepdims=True))
    alpha = jnp.exp(m_prev - m_new)
    pr = jnp.exp(s - m_new)
    l_ref[...] = l_ref[...] * alpha + jnp.sum(pr, axis=-1, keepdims=True)
    m_ref[...] = m_new
    pb = pr.astype(BF16)
    pv = jnp.zeros(acc_ref.shape, F32)
    for h in range(H_DA):
        t = None
        for u, vc_ref in enumerate(vc_refs):
            vh = vc_ref[0, :, h * DV_DA:(h + 1) * DV_DA].astype(BF16)
            d = jnp.dot(pb[:, u * LANES:(u + 1) * LANES], vh, preferred_element_type=F32)
            t = d if t is None else t + d
        pv = jnp.where(rowid % H_DA == h, t, pv)
    acc_ref[...] = acc_ref[...] * alpha + pv

    @pl.when(step == pl.num_programs(1) - 1)
    def _finish():
        lam = _lam_value(lq_ref, lk_ref, lam_init)
        on = acc_ref[...] / l_ref[...]
        o = on[0:H_DA] - lam * on[H_DA:n_maps]
        o_ref[0] = _rms(o) * gs_ref[...] * (1.0 - lam_init)


def _attn_sample(q, k_new, v_new, cache_k, cache_v, layer, page_table, lam_q, lam_k, g_sub, lam_init):
    B = q.shape[0]
    n_pages = page_table.shape[1]
    depth, n_pool, page = cache_k.shape[0], cache_k.shape[1], cache_k.shape[2]
    assert page == LANES
    n_maps = 2 * H_DA
    pg = 4 if n_pages % 4 == 0 else 1
    first_page = layer * n_pool

    def page_spec(width, u):
        return pl.BlockSpec((1, page, width),
                            lambda b, s, pt: (first_page + pt[b * n_pages + s * pg + u], 0, 0))

    grid_spec = pltpu.PrefetchScalarGridSpec(
        num_scalar_prefetch=1,
        grid=(B, n_pages // pg),
        in_specs=[pl.BlockSpec((2, DK_DA), lambda b, s, pt: (0, 0)),
                  pl.BlockSpec((2, DK_DA), lambda b, s, pt: (0, 0)),
                  pl.BlockSpec((1, DV_DA), lambda b, s, pt: (0, 0)),
                  pl.BlockSpec((1, n_maps, DK_DA), lambda b, s, pt: (b, 0, 0)),
                  pl.BlockSpec((1, n_maps, DK_DA), lambda b, s, pt: (b, 0, 0)),
                  pl.BlockSpec((1, H_DA, DV_DA), lambda b, s, pt: (b, 0, 0))]
                 + [page_spec(n_maps * DK_DA, u) for u in range(pg)]
                 + [page_spec(W_DA, u) for u in range(pg)],
        out_specs=pl.BlockSpec((1, H_DA, DV_DA), lambda b, s, pt: (b, 0, 0)),
        scratch_shapes=[pltpu.VMEM((n_maps, 1), F32), pltpu.VMEM((n_maps, 1), F32),
                        pltpu.VMEM((n_maps, DV_DA), F32)],
    )
    kc = cache_k.reshape(depth * n_pool, page, n_maps * DK_DA)
    vc = cache_v.reshape(depth * n_pool, page, W_DA)
    return pl.pallas_call(
        functools.partial(_attn_s_kernel, pg=pg, scale=DK_DA ** -0.5, lam_init=lam_init),
        out_shape=SDS((B, H_DA, DV_DA), F32),
        grid_spec=grid_spec,
        compiler_params=_cparams(("arbitrary", "arbitrary"), 40),
        name="attn_sample",
    )(page_table.reshape(-1), lam_q, lam_k, g_sub.reshape(1, DV_DA), q, k_new, v_new,
      *([kc] * pg), *([vc] * pg))


def _ret_decay_tables(chunk):
    lg = jnp.log1p(-jnp.exp2(-5.0 - jnp.arange(H_RET, dtype=F32)))
    idx = jnp.arange(chunk, dtype=F32)
    diff = idx[:, None] - idx[None, :]
    causal = diff >= 0
    dmask = jnp.where(causal, jnp.exp(jnp.where(causal, diff, 0.0)[None] * lg[:, None, None]), 0.0)
    qdec = jnp.exp((idx + 1.0)[None, :] * lg[:, None])[..., None]
    kdec = jnp.exp((chunk - 1.0 - idx)[None, :] * lg[:, None])[..., None]
    sdec = jnp.exp(chunk * lg)[:, None, None]
    return dmask, qdec, kdec, sdec


def _ret_prompt_kernel(dm_ref, qd_ref, kd_ref, sd_ref, gr_ref, q_ref, k_ref, v_ref, g_ref,
                       o_ref, st_ref, state_ref):
    c = pl.program_id(1)

    @pl.when(c == 0)
    def _zero_state():
        state_ref[...] = jnp.zeros(state_ref.shape, F32)

    q = q_ref[...].astype(BF16)
    k = k_ref[...]
    v = v_ref[...].astype(BF16)
    st = state_ref[...]
    qk = lax.dot_general(q, k.astype(BF16), NT_DIMS, preferred_element_type=F32) * dm_ref[0]
    inner = jnp.dot(qk.astype(BF16), v, preferred_element_type=F32)
    cross = jnp.dot(q, st.astype(BF16), preferred_element_type=F32) * qd_ref[0]
    kdecayed = (k * kd_ref[0]).astype(BF16)
    state_ref[...] = sd_ref[0] * st + lax.dot_general(kdecayed, v, TN_DIMS, preferred_element_type=F32)
    o = inner + cross
    o_ref[...] = (_rms(o) * gr_ref[...] * _silu(g_ref[...])).astype(o_ref.dtype)

    @pl.when(c == pl.num_programs(1) - 1)
    def _emit_state():
        st_ref[0] = state_ref[...]


def _ret_prompt(z, g_ret):
    S = z.shape[0]
    C = RET_CHUNK
    dmask, qdec, kdec, sdec = _ret_decay_tables(C)
    per_head = lambda shape: pl.BlockSpec((1,) + shape, lambda h, c: (h, 0, 0))
    return pl.pallas_call(
        _ret_prompt_kernel,
        out_shape=(SDS((S, W_RET), BF16), SDS((H_RET, DK_RET, DV_RET), F32)),
        grid=(H_RET, S // C),
        in_specs=[per_head((C, C)), per_head((C, 1)), per_head((C, 1)), per_head((1, 1)),
                  pl.BlockSpec((1, DV_RET), lambda h, c: (0, 0)),
                  pl.BlockSpec((C, DK_RET), lambda h, c: (c, OFF_RQ // DK_RET + h)),
                  pl.BlockSpec((C, DK_RET), lambda h, c: (c, OFF_RK // DK_RET + h)),
                  pl.BlockSpec((C, DV_RET), lambda h, c: (c, OFF_RV // DV_RET + h)),
                  pl.BlockSpec((C, DV_RET), lambda h, c: (c, OFF_RG // DV_RET + h))],
        out_specs=(pl.BlockSpec((C, DV_RET), lambda h, c: (c, h)),
                   pl.BlockSpec((1, DK_RET, DV_RET), lambda h, c: (h, 0, 0))),
        scratch_shapes=[pltpu.VMEM((DK_RET, DV_RET), F32)],
        compiler_params=_cparams(("arbitrary", "arbitrary"), 32),
        name="ret_prompt",
    )(dmask, qdec, kdec, sdec, g_ret.reshape(1, DV_RET), z, z, z, z)


def _ret_sample_kernel(dm_ref, qd_ref, kd_ref, sd_ref, gr_ref, q_ref, k_ref, v_ref, g_ref, s_ref,
                       o_ref, so_ref):
    q = q_ref[0]
    k = k_ref[0]
    v = v_ref[0]
    rowid = lax.broadcasted_iota(I32, (H_RET, 1), 0)
    qb = q.astype(BF16)
    vb = v.astype(BF16)
    kdecayed = k * kd_ref[...]
    sd = sd_ref[...]
    cross = jnp.zeros(v.shape, F32)
    for h in range(H_RET):
        st = s_ref[0, h]
        cross = jnp.where(rowid == h, jnp.dot(qb, st.astype(BF16), preferred_element_type=F32), cross)
        kh = jnp.where(rowid == h, kdecayed, 0.0).astype(BF16)
        so_ref[0, h] = sd[h:h + 1, :] * st + lax.dot_general(kh, vb, TN_DIMS, preferred_element_type=F32)
    qk = jnp.sum(q * k, axis=-1, keepdims=True) * dm_ref[...]
    o = qk * v + cross * qd_ref[...]
    o_ref[0] = _rms(o) * gr_ref[...] * _silu(g_ref[0])


def _ret_sample(q, k, v, g, state_ret, layer, g_ret):
    B = q.shape[0]
    dmask, qdec, kdec, sdec = _ret_decay_tables(1)
    head_vec = pl.BlockSpec((H_RET, 1), lambda b: (0, 0))
    row3 = lambda w: pl.BlockSpec((1, H_RET, w), lambda b: (b, 0, 0))
    return pl.pallas_call(
        _ret_sample_kernel,
        out_shape=(SDS((B, H_RET, DV_RET), F32), SDS((B, H_RET, DK_RET, DV_RET), F32)),
        grid=(B,),
        in_specs=[head_vec, head_vec, head_vec, head_vec,
                  pl.BlockSpec((1, DV_RET), lambda b: (0, 0)),
                  row3(DK_RET), row3(DK_RET), row3(DV_RET), row3(DV_RET),
                  pl.BlockSpec((None, 1, H_RET, DK_RET, DV_RET), lambda b: (layer, b, 0, 0, 0))],
        out_specs=(row3(DV_RET), pl.BlockSpec((1, H_RET, DK_RET, DV_RET), lambda b: (b, 0, 0, 0))),
        compiler_params=_cparams(("arbitrary",), 32),
        name="ret_sample",
    )(dmask.reshape(H_RET, 1), qdec.reshape(H_RET, 1), kdec.reshape(H_RET, 1), sdec.reshape(H_RET, 1),
      g_ret.reshape(1, DV_RET), q, k, v, g, state_ret)


def _outproj_kernel(oa_ref, ob_ref, w_ref, x_ref, gt_ref, o_ref, wbf_ref):
    @pl.when(pl.program_id(1) == 0)
    def _cast_weights():
        wbf_ref[...] = w_ref[...].astype(BF16)

    acc = jnp.dot(oa_ref[...].astype(BF16), wbf_ref[0:W_DA, :], preferred_element_type=F32)
    acc += jnp.dot(ob_ref[...].astype(BF16), wbf_ref[W_DA:W_DA + W_RET, :], preferred_element_type=F32)
    o_ref[...] = x_ref[...] + gt_ref[...] * acc


def _outproj(o_da, o_ret, w_out, layer, x, mod, gt_col, tm):
    M, D = x.shape
    tn = 512
    gt_spec = (pl.BlockSpec((1, tn), lambda j, i: (0, gt_col * (D // tn) + j)) if mod.shape[0] == 1
               else pl.BlockSpec((tm, tn), lambda j, i: (i, gt_col * (D // tn) + j)))
    return pl.pallas_call(
        _outproj_kernel,
        out_shape=SDS((M, D), F32),
        grid=(D // tn, M // tm),
        in_specs=[pl.BlockSpec((tm, W_DA), lambda j, i: (i, 0)),
                  pl.BlockSpec((tm, W_RET), lambda j, i: (i, 0)),
                  pl.BlockSpec((None, W_DA + W_RET, tn), lambda j, i: (layer, 0, j)),
                  pl.BlockSpec((tm, tn), lambda j, i: (i, j)),
                  gt_spec],
        out_specs=pl.BlockSpec((tm, tn), lambda j, i: (i, j)),
        scratch_shapes=[pltpu.VMEM((W_DA + W_RET, tn), BF16)],
        compiler_params=_cparams(("arbitrary", "arbitrary"), 56),
        name="outproj",
    )(o_da, o_ret, w_out, x, mod)


def _router_kernel(x_ref, g_ref, sc_ref, sh_ref, wr_ref, br_ref, h_ref, ti_ref, tg_ref):
    h = (_rms(x_ref[...]) * g_ref[...]) * (1.0 + sc_ref[...]) + sh_ref[...]
    h_ref[...] = h
    hb = h.astype(BF16)
    hl = (h - hb.astype(F32)).astype(BF16)
    w = wr_ref[...]
    wb = w.astype(BF16)
    wl = (w - wb.astype(F32)).astype(BF16)
    logits = (jnp.dot(hb, wb, preferred_element_type=F32) + jnp.dot(hb, wl, preferred_element_type=F32)
              + jnp.dot(hl, wb, preferred_element_type=F32)) + br_ref[...]
    lane = lax.broadcasted_iota(I32, logits.shape, 1)
    vals, idxs = [], []
    for _ in range(TOP_K):
        m = jnp.max(logits, axis=-1, keepdims=True)
        idx = jnp.min(jnp.where(logits == m, lane, LANES), axis=-1, keepdims=True)
        vals.append(m)
        idxs.append(idx)
        logits = jnp.where(lane == idx, -jnp.inf, logits)
    exps = [jnp.exp(v - vals[0]) for v in vals]
    denom = exps[0] + exps[1] + exps[2] + exps[3]
    ti = jnp.zeros(lane.shape, I32)
    tg = jnp.zeros(lane.shape, F32)
    for kk in range(TOP_K):
        ti = jnp.where(lane == kk, idxs[kk], ti)
        tg = jnp.where(lane == kk, exps[kk] / denom, tg)
    ti_ref[...] = ti
    tg_ref[...] = tg


def _norm_router(x, g, mod, sh_col, sc_col, wr_pad, br_pad, tm):
    M, D = x.shape
    return pl.pallas_call(
        _router_kernel,
        out_shape=(SDS((M, D), F32), SDS((M, LANES), I32), SDS((M, LANES), F32)),
        grid=(M // tm,),
        in_specs=[pl.BlockSpec((tm, D), lambda i: (i, 0)),
                  pl.BlockSpec((1, D), lambda i: (0, 0)),
                  _mod_spec(mod, tm, sc_col), _mod_spec(mod, tm, sh_col),
                  pl.BlockSpec((D, LANES), lambda i: (0, 0)),
                  pl.BlockSpec((1, LANES), lambda i: (0, 0))],
        out_specs=(pl.BlockSpec((tm, D), lambda i: (i, 0)),
                   pl.BlockSpec((tm, LANES), lambda i: (i, 0)),
                   pl.BlockSpec((tm, LANES), lambda i: (i, 0))),
        compiler_params=_cparams(("arbitrary",), 48),
        name="norm_router",
    )(x, g.reshape(1, D), mod, mod, wr_pad, br_pad)


def _route_tables(top_i, tm):
    T = top_i.shape[0]
    A = T * TOP_K
    experts = jnp.arange(N_EXPERTS, dtype=I32)
    flat_e = top_i.reshape(A)
    onehot = (flat_e[:, None] == experts[None, :]).astype(I32)
    csum = jnp.cumsum(onehot, axis=0)
    rank = jnp.take_along_axis(csum, flat_e[:, None], axis=1)[:, 0] - 1
    counts = csum[-1]
    pcounts = (counts + tm - 1) // tm * tm
    pend = jnp.cumsum(pcounts)
    dest = (pend - pcounts)[flat_e] + rank
    n_blocks = (A + N_EXPERTS * (tm - 1) + tm - 1) // tm
    row_tok = jnp.zeros((n_blocks * tm,), I32).at[dest].set(jnp.arange(A, dtype=I32) // TOP_K)
    n_used = (pend[-1] // tm).astype(I32)
    blk = jnp.arange(n_blocks, dtype=I32)
    blk_e = jnp.minimum(jnp.searchsorted(pend, blk * tm, side='right'), N_EXPERTS - 1).astype(I32)
    blk_e = jnp.where(blk < n_used, blk_e, blk_e[jnp.maximum(n_used - 1, 0)])
    present = jnp.where(counts > 0, experts, N_EXPERTS)
    first_at_or_after = jnp.flip(lax.cummin(jnp.flip(present)))
    next_e = jnp.concatenate([first_at_or_after[1:], jnp.full((1,), N_EXPERTS, I32)])
    next_e = jnp.where(next_e >= N_EXPERTS, -1, next_e)
    return dest.astype(I32), row_tok, blk_e, next_e[blk_e].astype(I32), n_used.reshape(1)


def _start_row_copies(idx_ref, src_hbm, buf_ref, slot, sem, n):
    def body(r, carry):
        pltpu.make_async_copy(src_hbm.at[pl.ds(idx_ref[r], 1), :], buf_ref.at[slot, pl.ds(r, 1), :],
                              sem.at[slot]).start()
        return carry
    lax.fori_loop(0, n, body, 0)


def _wait_row_copies(src_hbm, buf_ref, slot, sem, n):
    def body(r, carry):
        pltpu.make_async_copy(src_hbm.at[pl.ds(0, 1), :], buf_ref.at[slot, pl.ds(r, 1), :],
                              sem.at[slot]).wait()
        return carry
    lax.fori_loop(0, n, body, 0)


def _gather_kernel(nu_ref, tok_ref, tok_next_ref, h_hbm, o_ref, buf_ref, sem, *, rows):
    i = pl.program_id(0)
    slot = i % 2

    @pl.when(i == 0)
    def _prime():
        _start_row_copies(tok_ref, h_hbm, buf_ref, 0, sem, rows)

    @pl.when(i + 1 < nu_ref[0])
    def _fetch_next():
        _start_row_copies(tok_next_ref, h_hbm, buf_ref, 1 - slot, sem, rows)

    @pl.when(i < nu_ref[0])
    def _emit():
        _wait_row_copies(h_hbm, buf_ref, slot, sem, rows)
        o_ref[...] = buf_ref[slot].astype(o_ref.dtype)

    @pl.when(i >= nu_ref[0])
    def _unused_block():
        o_ref[...] = jnp.zeros(o_ref.shape, o_ref.dtype)


def _gather_rows(h, row_tok, n_used, rows):
    P = row_tok.shape[0]
    D = h.shape[1]
    nb = P // rows
    return pl.pallas_call(
        functools.partial(_gather_kernel, rows=rows),
        out_shape=SDS((P, D), BF16),
        grid=(nb,),
        in_specs=[pl.BlockSpec((1,), lambda i: (0,), memory_space=pltpu.SMEM),
                  pl.BlockSpec((rows,), lambda i: (i,), memory_space=pltpu.SMEM),
                  pl.BlockSpec((rows,), lambda i: (jnp.minimum(i + 1, nb - 1),), memory_space=pltpu.SMEM),
                  pl.BlockSpec(memory_space=pl.ANY)],
        out_specs=pl.BlockSpec((rows, D), lambda i: (i, 0)),
        scratch_shapes=[pltpu.VMEM((2, rows, D), F32), pltpu.SemaphoreType.DMA((2,))],
        compiler_params=_cparams(("arbitrary",), 32),
        name="moe_gather",
    )(n_used, row_tok, row_tok, h)


def _new_expert(be_ref, i):
    return (i == 0) | (be_ref[i] != be_ref[jnp.maximum(i - 1, 0)])


def _expert_tile_stream(be_ref, nx_ref, w_hbm, wf_ref, sem, layer, tn, consume):
    j = pl.program_id(0)
    i = pl.program_id(1)

    def tile_copy(e, jj):
        cols = pl.ds(pl.multiple_of(jj * tn, tn), tn)
        return pltpu.make_async_copy(w_hbm.at[layer, e, :, cols], wf_ref, sem)

    @pl.when((j == 0) & (i == 0))
    def _prime():
        tile_copy(be_ref[0], 0).start()

    @pl.when(_new_expert(be_ref, i))
    def _next_weights():
        tile_copy(be_ref[i], j).wait()
        consume()
        nx = nx_ref[i]

        @pl.when(nx >= 0)
        def _same_tile_next_group():
            tile_copy(nx, j).start()

        @pl.when((nx < 0) & (j + 1 < pl.num_programs(0)))
        def _next_tile_first_group():
            tile_copy(be_ref[0], j + 1).start()


def _moe_gu_kernel(be_ref, nx_ref, nu_ref, x_ref, w_hbm, b_ref, o_ref, wf_ref, wbf_ref, sem, *, tn, layer):
    i = pl.program_id(1)

    def cast_weights():
        wbf_ref[...] = wf_ref[...].astype(BF16)

    _expert_tile_stream(be_ref, nx_ref, w_hbm, wf_ref, sem, layer, tn, cast_weights)

    @pl.when(i < nu_ref[0])
    def _compute():
        x = x_ref[...]
        even = (lax.broadcasted_iota(I32, (x.shape[0], LANES), 1) % 2) == 0
        for c in range(tn // MXU_N):
            cols = slice(c * MXU_N, (c + 1) * MXU_N)
            gu = jnp.dot(x, wbf_ref[:, cols], preferred_element_type=F32) + b_ref[0, :, cols]
            a = gu[:, :LANES]
            b = gu[:, LANES:]
            gate = jnp.where(even, a, pltpu.roll(b, 1, 1))
            up = jnp.where(even, pltpu.roll(a, LANES - 1, 1), b)
            gate = jnp.minimum(gate, SWIGLU_LIMIT)
            up = jnp.clip(up, -SWIGLU_LIMIT, SWIGLU_LIMIT)
            act = (up + 1.0) * gate * jax.nn.sigmoid(SWIGLU_ALPHA * gate)
            o_ref[:, c * LANES:(c + 1) * LANES] = act.astype(o_ref.dtype)

    @pl.when(i >= nu_ref[0])
    def _unused_block():
        o_ref[...] = jnp.zeros(o_ref.shape, o_ref.dtype)


def _moe_gate_up(xs, w_gu, b_gu, layer, blk_e, blk_next, n_used, tm):
    P, D = xs.shape
    F2 = w_gu.shape[3]
    tn = 1024
    grid_spec = pltpu.PrefetchScalarGridSpec(
        num_scalar_prefetch=3,
        grid=(F2 // tn, P // tm),
        in_specs=[pl.BlockSpec((tm, D), lambda j, i, be, nx, nu: (i, 0)),
                  pl.BlockSpec(memory_space=pl.ANY),
                  pl.BlockSpec((1, 1, tn), lambda j, i, be, nx, nu: (be[i], 0, j))],
        out_specs=pl.BlockSpec((tm, tn // 2), lambda j, i, be, nx, nu: (i, j)),
        scratch_shapes=[pltpu.VMEM((D, tn), F32), pltpu.VMEM((D, tn), BF16), pltpu.SemaphoreType.DMA(())],
    )
    return pl.pallas_call(
        functools.partial(_moe_gu_kernel, tn=tn, layer=layer),
        out_shape=SDS((P, F2 // 2), BF16),
        grid_spec=grid_spec,
        compiler_params=_cparams(("arbitrary", "arbitrary"), 48),
        name="moe_gate_up",
    )(blk_e, blk_next, n_used, xs, w_gu, b_gu[layer].reshape(N_EXPERTS, 1, F2))


def _moe_down_kernel(be_ref, nx_ref, nu_ref, a_ref, w_hbm, b_ref, o_ref, wf_ref, slab_ref, wbf_ref, sem,
                     *, tn, layer):
    i = pl.program_id(1)
    half = LANES // 2

    def cast_and_interleave_weights():
        for c in range(tn // LANES):
            cols = slice(c * LANES, (c + 1) * LANES)
            s = c % 2
            for g in range(wf_ref.shape[0] // LANES):
                slab_ref[s, pl.ds(g * LANES, half, stride=2), :] = wf_ref[g * LANES:g * LANES + half, cols]
                slab_ref[s, pl.ds(g * LANES + 1, half, stride=2), :] = (
                    wf_ref[g * LANES + half:(g + 1) * LANES, cols])
            wbf_ref[:, cols] = slab_ref[s].astype(BF16)

    _expert_tile_stream(be_ref, nx_ref, w_hbm, wf_ref, sem, layer, tn, cast_and_interleave_weights)

    @pl.when(i < nu_ref[0])
    def _compute():
        o_ref[...] = jnp.dot(a_ref[...], wbf_ref[...], preferred_element_type=F32) + b_ref[0]

    @pl.when(i >= nu_ref[0])
    def _unused_block():
        o_ref[...] = jnp.zeros(o_ref.shape, o_ref.dtype)


def _moe_down(act, w_down, b_down, layer, blk_e, blk_next, n_used, tm):
    P, F = act.shape
    D = w_down.shape[3]
    tn = 1024
    grid_spec = pltpu.PrefetchScalarGridSpec(
        num_scalar_prefetch=3,
        grid=(D // tn, P // tm),
        in_specs=[pl.BlockSpec((tm, F), lambda j, i, be, nx, nu: (i, 0)),
                  pl.BlockSpec(memory_space=pl.ANY),
                  pl.BlockSpec((1, 1, tn), lambda j, i, be, nx, nu: (be[i], 0, j))],
        out_specs=pl.BlockSpec((tm, tn), lambda j, i, be, nx, nu: (i, j)),
        scratch_shapes=[pltpu.VMEM((F, tn), F32), pltpu.VMEM((2, F, LANES), F32),
                        pltpu.VMEM((F, tn), BF16), pltpu.SemaphoreType.DMA(())],
    )
    return pl.pallas_call(
        functools.partial(_moe_down_kernel, tn=tn, layer=layer),
        out_shape=SDS((P, D), F32),
        grid_spec=grid_spec,
        compiler_params=_cparams(("arbitrary", "arbitrary"), 48),
        name="moe_down",
    )(blk_e, blk_next, n_used, act, w_down, b_down[layer].reshape(N_EXPERTS, 1, D))


def _combine_kernel(dest_ref, dest_next_ref, y_hbm, tg_ref, x_ref, gt_ref, o_ref, buf_ref, sem, *, tb):
    i = pl.program_id(0)
    slot = i % 2
    n = TOP_K * tb

    @pl.when(i == 0)
    def _prime():
        _start_row_copies(dest_ref, y_hbm, buf_ref, 0, sem, n)

    @pl.when(i + 1 < pl.num_programs(0))
    def _fetch_next():
        _start_row_copies(dest_next_ref, y_hbm, buf_ref, 1 - slot, sem, n)

    _wait_row_copies(y_hbm, buf_ref, slot, sem, n)
    tg = tg_ref[...]
    y = buf_ref[slot, 0:tb, :] * tg[:, 0:1]
    for kk in range(1, TOP_K):
        y += buf_ref[slot, kk * tb:(kk + 1) * tb, :] * tg[:, kk:kk + 1]
    o_ref[...] = x_ref[...] + gt_ref[...] * y


def _combine(yb, dest, gates, x, mod, gt_col, tb):
    M, D = x.shape
    nb = M // tb
    n = TOP_K * tb
    dest_by_k = dest.reshape(nb, tb, TOP_K).swapaxes(1, 2).reshape(-1)
    return pl.pallas_call(
        functools.partial(_combine_kernel, tb=tb),
        out_shape=SDS((M, D), F32),
        grid=(nb,),
        in_specs=[pl.BlockSpec((n,), lambda i: (i,), memory_space=pltpu.SMEM),
                  pl.BlockSpec((n,), lambda i: (jnp.minimum(i + 1, nb - 1),), memory_space=pltpu.SMEM),
                  pl.BlockSpec(memory_space=pl.ANY),
                  pl.BlockSpec((tb, LANES), lambda i: (i, 0)),
                  pl.BlockSpec((tb, D), lambda i: (i, 0)),
                  _mod_spec(mod, tb, gt_col)],
        out_specs=pl.BlockSpec((tb, D), lambda i: (i, 0)),
        scratch_shapes=[pltpu.VMEM((2, n, D), F32), pltpu.SemaphoreType.DMA((2,))],
        compiler_params=_cparams(("arbitrary",), 48),
        name="moe_combine",
    )(dest_by_k, dest_by_k, yb, gates, x, mod)


def kernel(x_prompt, x_sample, cache_k, cache_v, state_ret, page_table, c_prompt, c_sample,
           w_ada, b_ada, g_attn, w_in, g_qn, g_kn, lam_q, lam_k, g_sub, g_ret, w_out,
           g_ffn, w_router, b_router, w_gu, b_gu, w_down, b_down):
    depth = w_ada.shape[0]
    assert depth == 1 and x_prompt.shape[0] == 1 and x_sample.shape[1] == 1
    layer = 0
    S = x_prompt.shape[1]
    Bd = x_sample.shape[0]
    D = D_MODEL
    lam_init = 0.8 - 0.6 * math.exp(-0.3 * layer)
    past = page_table.shape[1] * cache_k.shape[2]
    tm_p = 1024
    tm_moe = 256

    xp = x_prompt.reshape(S, D)
    xs = x_sample.reshape(Bd, D)

    n_c = 1 + Bd
    n_c_pad = -(-n_c // 16) * 16
    c_all = jnp.concatenate([c_prompt, c_sample, jnp.zeros((n_c_pad - n_c, D), F32)], axis=0)
    mod = _ada_mod(c_all, w_ada, b_ada, layer)
    mod_p, mod_s = mod[0:1], mod[1:n_c]
    SH1, SC1, GT1, SH2, SC2, GT2 = range(N_ADA)

    tabs_p = _rot_tables(jnp.arange(S))
    tabs_s = _rot_tables(jnp.full((Bd,), past, I32))

    h_p = _norm_mod(xp, g_attn[layer], mod_p, SH1, SC1, 512)
    z_p = _inproj(h_p, w_in, layer, g_qn[layer], g_kn[layer], tabs_p, tm_p)
    o_da_p = _attn_prompt(z_p, lam_q[layer], lam_k[layer], g_sub[layer], lam_init)
    o_ret_p, st_p = _ret_prompt(z_p, g_ret[layer])
    x1_p = _outproj(o_da_p, o_ret_p, w_out, layer, xp, mod_p, GT1, tm_p)

    h_s = _norm_mod(xs, g_attn[layer], mod_s, SH1, SC1, Bd)
    z_s = _inproj(h_s, w_in, layer, g_qn[layer], g_kn[layer], tabs_s, Bd)
    n_maps = 2 * H_DA
    by_map = lambda a: a.reshape(Bd, H_DA, 2, DK_DA).swapaxes(1, 2).reshape(Bd, n_maps, DK_DA)
    o_da_s = _attn_sample(
        by_map(z_s[:, OFF_DQ:OFF_DK]), by_map(z_s[:, OFF_DK:OFF_DV]),
        z_s[:, OFF_DV:OFF_RQ].reshape(Bd, H_DA, DV_DA), cache_k, cache_v, layer, page_table,
        lam_q[layer], lam_k[layer], g_sub[layer], lam_init)
    o_ret_s, st_s = _ret_sample(
        z_s[:, OFF_RQ:OFF_RK].reshape(Bd, H_RET, DK_RET), z_s[:, OFF_RK:OFF_RV].reshape(Bd, H_RET, DK_RET),
        z_s[:, OFF_RV:OFF_RG].reshape(Bd, H_RET, DV_RET), z_s[:, OFF_RG:D_IN].reshape(Bd, H_RET, DV_RET),
        state_ret, layer, g_ret[layer])
    x1_s = _outproj(o_da_s.reshape(Bd, W_DA), o_ret_s.reshape(Bd, W_RET), w_out, layer, xs, mod_s, GT1, Bd)

    wr_pad = jnp.pad(w_router[layer], ((0, 0), (0, LANES - N_EXPERTS)))
    br_pad = jnp.pad(b_router[layer], (0, LANES - N_EXPERTS), constant_values=NEG_INF).reshape(1, LANES)
    h2_p, ti_p, tg_p = _norm_router(x1_p, g_ffn[layer], mod_p, SH2, SC2, wr_pad, br_pad, 512)
    h2_s, ti_s, tg_s = _norm_router(x1_s, g_ffn[layer], mod_s, SH2, SC2, wr_pad, br_pad, Bd)
    h2 = jnp.concatenate([h2_p, h2_s], axis=0)
    top_i = jnp.concatenate([ti_p[:, :TOP_K], ti_s[:, :TOP_K]], axis=0)
    dest, row_tok, blk_e, blk_next, n_used = _route_tables(top_i, tm_moe)
    xg = _gather_rows(h2, row_tok, n_used, tm_moe)
    act = _moe_gate_up(xg, w_gu, b_gu, layer, blk_e, blk_next, n_used, tm_moe)
    yb = _moe_down(act, w_down, b_down, layer, blk_e, blk_next, n_used, tm_moe)
    y_p = _combine(yb, dest[:S * TOP_K], tg_p, x1_p, mod_p, GT2, 128)
    y_s = _combine(yb, dest[S * TOP_K:], tg_s, x1_s, mod_s, GT2, Bd)

    return (y_p.reshape(1, S, D), y_s.reshape(Bd, 1, D),
            z_p[:, OFF_DK:OFF_DV].reshape(1, 1, S, H_DA, 2, DK_DA),
            z_p[:, OFF_DV:OFF_RQ].reshape(1, 1, S, H_DA, DV_DA),
            st_p.reshape(1, 1, H_RET, DK_RET, DV_RET),
            z_s[:, OFF_DK:OFF_DV].reshape(1, Bd, 1, H_DA, 2, DK_DA),
            z_s[:, OFF_DV:OFF_RQ].reshape(1, Bd, 1, H_DA, DV_DA),
            st_s.reshape(1, Bd, H_RET, DK_RET, DV_RET))
```

```python
import functools
import math

import numpy as np
import jax
import jax.numpy as jnp
from jax import lax
from jax.experimental import pallas as pl
from jax.experimental.pallas import tpu as pltpu

F32 = jnp.float32
BF16 = jnp.bfloat16
I32 = jnp.int32
SDS = jax.ShapeDtypeStruct

D_MODEL = 4096
H_DA, DK_DA, DV_DA = 8, 128, 256
H_RET, DK_RET, DV_RET = 8, 128, 256
W_DA = H_DA * DV_DA
W_RET = H_RET * DV_RET
OFF_DQ = 0
OFF_DK = OFF_DQ + H_DA * 2 * DK_DA
OFF_DV = OFF_DK + H_DA * 2 * DK_DA
OFF_RQ = OFF_DV + W_DA
OFF_RK = OFF_RQ + H_RET * DK_RET
OFF_RV = OFF_RK + H_RET * DK_RET
OFF_RG = OFF_RV + W_RET
D_IN = OFF_RG + W_RET
ROPE_THETA = 10000.0
RET_THETA = 10000.0
RET_CHUNK = 128
N_EXPERTS = 32
TOP_K = 4
SWIGLU_LIMIT = 7.0
SWIGLU_ALPHA = 1.702
NORM_EPS = 1e-6
NEG_INF = -1e30
N_ADA = 6
LOG2E = 1.4426950408889634
LANES = 128
MXU_N = 256
MIB = 1024 * 1024

NT_DIMS = (((1,), (1,)), ((), ()))
TN_DIMS = (((0,), (0,)), ((), ()))


def _cparams(sem, vmem_mib):
    return pltpu.CompilerParams(dimension_semantics=sem, vmem_limit_bytes=vmem_mib * MIB)


def _rms(x):
    return x * lax.rsqrt(jnp.mean(x * x, axis=-1, keepdims=True) + NORM_EPS)


def _silu(x):
    return x * jax.nn.sigmoid(x)


def _lane_tile(x, width):
    return jnp.concatenate([x] * (width // LANES), axis=1)


def _ada_kernel(c_ref, w_ref, b_ref, o_ref):
    a = _silu(c_ref[...]).astype(BF16)
    o_ref[...] = jnp.dot(a, w_ref[...].astype(BF16), preferred_element_type=F32) + b_ref[...]


def _ada_mod(c_all, w_ada, b_ada, layer):
    R, D = c_all.shape
    N = w_ada.shape[2]
    tn = 512
    return pl.pallas_call(
        _ada_kernel,
        out_shape=SDS((R, N), F32),
        grid=(N // tn,),
        in_specs=[pl.BlockSpec((R, D), lambda j: (0, 0)),
                  pl.BlockSpec((None, D, tn), lambda j: (layer, 0, j)),
                  pl.BlockSpec((1, tn), lambda j: (layer, j))],
        out_specs=pl.BlockSpec((R, tn), lambda j: (0, j)),
        compiler_params=_cparams(("arbitrary",), 40),
        name="ada_mod",
    )(c_all, w_ada, b_ada)


def _mod_spec(mod, tm, col):
    if mod.shape[0] == 1:
        return pl.BlockSpec((1, D_MODEL), lambda i: (0, col))
    return pl.BlockSpec((tm, D_MODEL), lambda i: (i, col))


def _norm_mod_kernel(x_ref, g_ref, sc_ref, sh_ref, o_ref):
    h = (_rms(x_ref[...]) * g_ref[...]) * (1.0 + sc_ref[...]) + sh_ref[...]
    o_ref[...] = h.astype(o_ref.dtype)


def _norm_mod(x, g, mod, sh_col, sc_col, tm):
    M, D = x.shape
    return pl.pallas_call(
        _norm_mod_kernel,
        out_shape=SDS((M, D), BF16),
        grid=(M // tm,),
        in_specs=[pl.BlockSpec((tm, D), lambda i: (i, 0)),
                  pl.BlockSpec((1, D), lambda i: (0, 0)),
                  _mod_spec(mod, tm, sc_col), _mod_spec(mod, tm, sh_col)],
        out_specs=pl.BlockSpec((tm, D), lambda i: (i, 0)),
        compiler_params=_cparams(("arbitrary",), 48),
        name="norm_mod",
    )(x, g.reshape(1, D), mod, mod)


def _inproj_kernel(h_ref, w_ref, gq_ref, gk_ref, cf_ref, sf_ref, cr_ref, sr_ref, o_ref, wbf_ref, *, tn):
    j = pl.program_id(0)
    i = pl.program_id(1)

    @pl.when(i == 0)
    def _cast_weights():
        wbf_ref[...] = w_ref[...].astype(BF16)

    o_ref[...] = jnp.dot(h_ref[...], wbf_ref[...], preferred_element_type=F32)
    groups = [slice(c * LANES, (c + 1) * LANES) for c in range(tn // LANES)]

    @pl.when(j < OFF_DV // tn)
    def _qk_norm_rope():
        g = jnp.where(j < OFF_DK // tn, gq_ref[...], gk_ref[...])
        cf = cf_ref[...]
        sf = sf_ref[...]
        for sl in groups:
            y = _rms(o_ref[:, sl]) * g
            o_ref[:, sl] = y * cf + pltpu.roll(y, DK_DA // 2, 1) * sf

    @pl.when((j >= OFF_RQ // tn) & (j < OFF_RV // tn))
    def _ret_rotate():
        scale = jnp.where(j >= OFF_RK // tn, DK_RET ** -0.5, 1.0).astype(F32)
        cr = cr_ref[...]
        sr = sr_ref[...]
        even = (lax.broadcasted_iota(I32, cr.shape, 1) % 2) == 0
        for sl in groups:
            x = o_ref[:, sl]
            swapped = jnp.where(even, pltpu.roll(x, LANES - 1, 1), pltpu.roll(x, 1, 1))
            o_ref[:, sl] = (x * cr + swapped * sr) * scale


def _inproj(h, w_in, layer, g_qn, g_kn, tabs, tm):
    M, D = h.shape
    tn = 512
    tab_spec = pl.BlockSpec((tm, LANES), lambda j, i: (i, 0))
    vec_spec = pl.BlockSpec((1, LANES), lambda j, i: (0, 0))
    return pl.pallas_call(
        functools.partial(_inproj_kernel, tn=tn),
        out_shape=SDS((M, D_IN), F32),
        grid=(D_IN // tn, M // tm),
        in_specs=[pl.BlockSpec((tm, D), lambda j, i: (i, 0)),
                  pl.BlockSpec((None, D, tn), lambda j, i: (layer, 0, j)),
                  vec_spec, vec_spec, tab_spec, tab_spec, tab_spec, tab_spec],
        out_specs=pl.BlockSpec((tm, tn), lambda j, i: (i, j)),
        scratch_shapes=[pltpu.VMEM((D, tn), BF16)],
        compiler_params=_cparams(("arbitrary", "arbitrary"), 56),
        name="inproj",
    )(h, w_in, g_qn.reshape(1, LANES), g_kn.reshape(1, LANES), *tabs)


def _rot_tables(pos):
    half = DK_DA // 2
    posf = pos.astype(F32)[:, None]
    inv = ROPE_THETA ** (-jnp.arange(half, dtype=F32) / half)
    ang = posf * inv[None, :]
    cos, sin = jnp.cos(ang), jnp.sin(ang)
    cf = jnp.concatenate([cos, cos], axis=-1)
    sf = jnp.concatenate([-sin, sin], axis=-1)
    inv_r = 1.0 / (RET_THETA ** jnp.linspace(0.0, 1.0, DK_RET // 2, dtype=F32))
    ang_r = posf * inv_r[None, :]
    cos_r, sin_r = jnp.cos(ang_r), jnp.sin(ang_r)
    cr = jnp.repeat(cos_r, 2, axis=-1)
    sr = jnp.stack([-sin_r, sin_r], axis=-1).reshape(pos.shape[0], DK_RET)
    return cf, sf, cr, sr


def _lam_value(lq_ref, lk_ref, lam_init):
    e = jnp.exp(jnp.sum(lq_ref[...] * lk_ref[...], axis=-1, keepdims=True))
    return e[0:1] - e[1:2] + lam_init


def _attn_kernel(qi_ref, kj_ref, lq_ref, lk_ref, gs_ref, q_ref, k_ref, v_ref, o_ref,
                 qbf_ref, m_ref, l_ref, acc_ref, *, tq, qscale, lam_init):
    t = pl.program_id(1)
    qi = qi_ref[t]
    kj = kj_ref[t]

    @pl.when(kj == 0)
    def _init():
        qbf_ref[...] = (q_ref[...] * qscale).astype(BF16)
        m_ref[...] = jnp.full(m_ref.shape, NEG_INF, F32)
        l_ref[...] = jnp.zeros(l_ref.shape, F32)
        acc_ref[...] = jnp.zeros(acc_ref.shape, F32)

    def step(on_diagonal):
        kb = k_ref[...].astype(BF16)
        vb = v_ref[...].astype(BF16)
        if on_diagonal:
            keep = lax.broadcasted_iota(I32, (tq, tq), 1) <= lax.broadcasted_iota(I32, (tq, tq), 0)
        for mm in range(2):
            sl = slice(mm * DK_DA, (mm + 1) * DK_DA)
            s = lax.dot_general(qbf_ref[:, sl], kb[:, sl], NT_DIMS, preferred_element_type=F32)
            if on_diagonal:
                s = jnp.where(keep, s, NEG_INF)
            m_prev = m_ref[mm]
            m_new = jnp.maximum(m_prev, jnp.max(s, axis=-1, keepdims=True))
            alpha = jnp.exp2(m_prev - m_new)
            p = jnp.exp2(s - _lane_tile(m_new, tq))
            l_ref[mm] = l_ref[mm] * alpha + jnp.sum(p, axis=-1, keepdims=True)
            acc_ref[mm] = (acc_ref[mm] * _lane_tile(alpha, DV_DA)
                           + jnp.dot(p.astype(BF16), vb, preferred_element_type=F32))
            m_ref[mm] = m_new

    @pl.when(kj < qi)
    def _below_diagonal():
        step(False)

    @pl.when(kj == qi)
    def _diagonal_and_finish():
        step(True)
        lam = _lam_value(lq_ref, lk_ref, lam_init)
        o = (acc_ref[0] / _lane_tile(l_ref[0], DV_DA)
             - lam * (acc_ref[1] / _lane_tile(l_ref[1], DV_DA)))
        o_ref[...] = (_rms(o) * gs_ref[...] * (1.0 - lam_init)).astype(o_ref.dtype)


def _attn_prompt(z, lam_q, lam_k, g_sub, lam_init):
    S = z.shape[0]
    tq = 512
    nb = S // tq
    wq = 2 * DK_DA
    pairs = [(qi, kj) for qi in range(nb) for kj in range(qi + 1)]
    qi_tab = jnp.asarray(np.array([p[0] for p in pairs], np.int32))
    kj_tab = jnp.asarray(np.array([p[1] for p in pairs], np.int32))
    grid_spec = pltpu.PrefetchScalarGridSpec(
        num_scalar_prefetch=2,
        grid=(H_DA, len(pairs)),
        in_specs=[pl.BlockSpec((2, DK_DA), lambda h, t, qt, kt: (0, 0)),
                  pl.BlockSpec((2, DK_DA), lambda h, t, qt, kt: (0, 0)),
                  pl.BlockSpec((1, DV_DA), lambda h, t, qt, kt: (0, 0)),
                  pl.BlockSpec((tq, wq), lambda h, t, qt, kt: (qt[t], OFF_DQ // wq + h)),
                  pl.BlockSpec((tq, wq), lambda h, t, qt, kt: (kt[t], OFF_DK // wq + h)),
                  pl.BlockSpec((tq, DV_DA), lambda h, t, qt, kt: (kt[t], OFF_DV // DV_DA + h))],
        out_specs=pl.BlockSpec((tq, DV_DA), lambda h, t, qt, kt: (qt[t], h)),
        scratch_shapes=[pltpu.VMEM((tq, wq), BF16),
                        pltpu.VMEM((2, tq, LANES), F32), pltpu.VMEM((2, tq, LANES), F32),
                        pltpu.VMEM((2, tq, DV_DA), F32)],
    )
    return pl.pallas_call(
        functools.partial(_attn_kernel, tq=tq, qscale=DK_DA ** -0.5 * LOG2E, lam_init=lam_init),
        out_shape=SDS((S, W_DA), BF16),
        grid_spec=grid_spec,
        compiler_params=_cparams(("arbitrary", "arbitrary"), 32),
        name="attn_prompt",
    )(qi_tab, kj_tab, lam_q, lam_k, g_sub.reshape(1, DV_DA), z, z, z)


def _attn_s_kernel(pt_ref, lq_ref, lk_ref, gs_ref, q_ref, kn_ref, vn_ref, *rest, pg, scale, lam_init):
    kc_refs = rest[:pg]
    vc_refs = rest[pg:2 * pg]
    o_ref, m_ref, l_ref, acc_ref = rest[2 * pg:]
    step = pl.program_id(1)
    n_maps = 2 * H_DA
    q = q_ref[0]
    qb = q.astype(BF16)
    rowid = lax.broadcasted_iota(I32, (n_maps, 1), 0)

    @pl.when(step == 0)
    def _init_with_new_token():
        m_ref[...] = jnp.sum(q * kn_ref[0], axis=-1, keepdims=True) * scale
        l_ref[...] = jnp.ones(l_ref.shape, F32)
        vn = vn_ref[0]
        acc = jnp.zeros(acc_ref.shape, F32)
        for h in range(H_DA):
            acc = jnp.where(rowid % H_DA == h, vn[h:h + 1, :], acc)
        acc_ref[...] = acc

    page = kc_refs[0].shape[1] // n_maps
    parts = []
    for kc_ref in kc_refs:
        s = jnp.zeros((n_maps, LANES), F32)
        for r in range(n_maps):
            krow = (r % H_DA) * 2 + r // H_DA
            kr = kc_ref[0, pl.ds(krow, page, stride=n_maps), :].astype(BF16)
            s = jnp.where(rowid == r, lax.dot_general(qb, kr, NT_DIMS, preferred_element_type=F32), s)
        parts.append(s)
    s = jnp.concatenate(parts, axis=1) * scale
    m_prev = m_ref[...]
    m_new = jnp.maximum(m_prev, jnp.max(s, axis=-1, keepdims=True))
    alpha = jnp.exp(m_prev - m_new)
    pr = jnp.exp(s - m_new)
    l_ref[...] = l_ref[...] * alpha + jnp.sum(pr, axis=-1, keepdims=True)
    m_ref[...] = m_new
    pb = pr.astype(BF16)
    pv_halves = []
    for half in range(DV_DA // LANES):
        pv = jnp.zeros((n_maps, LANES), F32)
        for h in range(H_DA):
            t = None
            for u, vc_ref in enumerate(vc_refs):
                vh = vc_ref[0, pl.ds(half * H_DA + h, page, stride=n_maps), :].astype(BF16)
                d = jnp.dot(pb[:, u * page:(u + 1) * page], vh, preferred_element_type=F32)
                t = d if t is None else t + d
            pv = jnp.where(rowid % H_DA == h, t, pv)
        pv_halves.append(pv)
    acc_ref[...] = acc_ref[...] * alpha + jnp.concatenate(pv_halves, axis=1)

    @pl.when(step == pl.num_programs(1) - 1)
    def _finish():
        lam = _lam_value(lq_ref, lk_ref, lam_init)
        on = acc_ref[...] / l_ref[...]
        o = on[0:H_DA] - lam * on[H_DA:n_maps]
        o_ref[0] = _rms(o) * gs_ref[...] * (1.0 - lam_init)


def _attn_sample(q, k_new, v_new, cache_k, cache_v, layer, page_table, lam_q, lam_k, g_sub, lam_init):
    B = q.shape[0]
    n_pages = page_table.shape[1]
    depth, n_pool, page = cache_k.shape[0], cache_k.shape[1], cache_k.shape[2]
    assert page == LANES
    n_maps = 2 * H_DA
    pg = 4 if n_pages % 4 == 0 else 1
    first_page = layer * n_pool

    def page_spec(u):
        return pl.BlockSpec((1, page * n_maps, LANES),
                            lambda b, s, pt: (first_page + pt[b * n_pages + s * pg + u], 0, 0))

    grid_spec = pltpu.PrefetchScalarGridSpec(
        num_scalar_prefetch=1,
        grid=(B, n_pages // pg),
        in_specs=[pl.BlockSpec((2, DK_DA), lambda b, s, pt: (0, 0)),
                  pl.BlockSpec((2, DK_DA), lambda b, s, pt: (0, 0)),
                  pl.BlockSpec((1, DV_DA), lambda b, s, pt: (0, 0)),
                  pl.BlockSpec((1, n_maps, DK_DA), lambda b, s, pt: (b, 0, 0)),
                  pl.BlockSpec((1, n_maps, DK_DA), lambda b, s, pt: (b, 0, 0)),
                  pl.BlockSpec((1, H_DA, DV_DA), lambda b, s, pt: (b, 0, 0))]
                 + [page_spec(u) for u in range(pg)] * 2,
        out_specs=pl.BlockSpec((1, H_DA, DV_DA), lambda b, s, pt: (b, 0, 0)),
        scratch_shapes=[pltpu.VMEM((n_maps, 1), F32), pltpu.VMEM((n_maps, 1), F32),
                        pltpu.VMEM((n_maps, DV_DA), F32)],
    )
    kc = cache_k.reshape(depth * n_pool, page * n_maps, DK_DA)
    vc = (cache_v.reshape(depth * n_pool, page, H_DA, DV_DA // LANES, LANES).transpose(0, 1, 3, 2, 4)
          .reshape(depth * n_pool, page * n_maps, LANES))
    return pl.pallas_call(
        functools.partial(_attn_s_kernel, pg=pg, scale=DK_DA ** -0.5, lam_init=lam_init),
        out_shape=SDS((B, H_DA, DV_DA), F32),
        grid_spec=grid_spec,
        compiler_params=_cparams(("arbitrary", "arbitrary"), 40),
        name="attn_sample",
    )(page_table.reshape(-1), lam_q, lam_k, g_sub.reshape(1, DV_DA), q, k_new, v_new,
      *([kc] * pg), *([vc] * pg))


def _ret_decay_tables(chunk):
    lg = jnp.log1p(-jnp.exp2(-5.0 - jnp.arange(H_RET, dtype=F32)))
    idx = jnp.arange(chunk, dtype=F32)
    diff = idx[:, None] - idx[None, :]
    causal = diff >= 0
    dmask = jnp.where(causal, jnp.exp(jnp.where(causal, diff, 0.0)[None] * lg[:, None, None]), 0.0)
    qdec = jnp.exp((idx + 1.0)[None, :] * lg[:, None])[..., None]
    kdec = jnp.exp((chunk - 1.0 - idx)[None, :] * lg[:, None])[..., None]
    sdec = jnp.exp(chunk * lg)[:, None, None]
    return dmask, qdec, kdec, sdec


def _ret_prompt_kernel(dm_ref, qd_ref, kd_ref, sd_ref, gr_ref, q_ref, k_ref, v_ref, g_ref,
                       o_ref, st_ref, state_ref):
    c = pl.program_id(1)

    @pl.when(c == 0)
    def _zero_state():
        state_ref[...] = jnp.zeros(state_ref.shape, F32)

    q = q_ref[...].astype(BF16)
    k = k_ref[...]
    v = v_ref[...].astype(BF16)
    st = state_ref[...]
    qk = lax.dot_general(q, k.astype(BF16), NT_DIMS, preferred_element_type=F32) * dm_ref[0]
    inner = jnp.dot(qk.astype(BF16), v, preferred_element_type=F32)
    cross = jnp.dot(q, st.astype(BF16), preferred_element_type=F32) * qd_ref[0]
    kdecayed = (k * kd_ref[0]).astype(BF16)
    state_ref[...] = sd_ref[0] * st + lax.dot_general(kdecayed, v, TN_DIMS, preferred_element_type=F32)
    o = inner + cross
    o_ref[...] = (_rms(o) * gr_ref[...] * _silu(g_ref[...])).astype(o_ref.dtype)

    @pl.when(c == pl.num_programs(1) - 1)
    def _emit_state():
        st_ref[0] = state_ref[...]


def _ret_prompt(z, g_ret):
    S = z.shape[0]
    C = RET_CHUNK
    dmask, qdec, kdec, sdec = _ret_decay_tables(C)
    per_head = lambda shape: pl.BlockSpec((1,) + shape, lambda h, c: (h, 0, 0))
    return pl.pallas_call(
        _ret_prompt_kernel,
        out_shape=(SDS((S, W_RET), BF16), SDS((H_RET, DK_RET, DV_RET), F32)),
        grid=(H_RET, S // C),
        in_specs=[per_head((C, C)), per_head((C, 1)), per_head((C, 1)), per_head((1, 1)),
                  pl.BlockSpec((1, DV_RET), lambda h, c: (0, 0)),
                  pl.BlockSpec((C, DK_RET), lambda h, c: (c, OFF_RQ // DK_RET + h)),
                  pl.BlockSpec((C, DK_RET), lambda h, c: (c, OFF_RK // DK_RET + h)),
                  pl.BlockSpec((C, DV_RET), lambda h, c: (c, OFF_RV // DV_RET + h)),
                  pl.BlockSpec((C, DV_RET), lambda h, c: (c, OFF_RG // DV_RET + h))],
        out_specs=(pl.BlockSpec((C, DV_RET), lambda h, c: (c, h)),
                   pl.BlockSpec((1, DK_RET, DV_RET), lambda h, c: (h, 0, 0))),
        scratch_shapes=[pltpu.VMEM((DK_RET, DV_RET), F32)],
        compiler_params=_cparams(("arbitrary", "arbitrary"), 32),
        name="ret_prompt",
    )(dmask, qdec, kdec, sdec, g_ret.reshape(1, DV_RET), z, z, z, z)


def _ret_sample_kernel(dm_ref, qd_ref, kd_ref, sd_ref, gr_ref, q_ref, k_ref, v_ref, g_ref, s_ref,
                       o_ref, so_ref):
    q = q_ref[0]
    k = k_ref[0]
    v = v_ref[0]
    rowid = lax.broadcasted_iota(I32, (H_RET, 1), 0)
    qb = q.astype(BF16)
    vb = v.astype(BF16)
    kdecayed = k * kd_ref[...]
    sd = sd_ref[...]
    cross = jnp.zeros(v.shape, F32)
    for h in range(H_RET):
        st = s_ref[0, h]
        cross = jnp.where(rowid == h, jnp.dot(qb, st.astype(BF16), preferred_element_type=F32), cross)
        kh = jnp.where(rowid == h, kdecayed, 0.0).astype(BF16)
        so_ref[0, h] = sd[h:h + 1, :] * st + lax.dot_general(kh, vb, TN_DIMS, preferred_element_type=F32)
    qk = jnp.sum(q * k, axis=-1, keepdims=True) * dm_ref[...]
    o = qk * v + cross * qd_ref[...]
    o_ref[0] = _rms(o) * gr_ref[...] * _silu(g_ref[0])


def _ret_sample(q, k, v, g, state_ret, layer, g_ret):
    B = q.shape[0]
    dmask, qdec, kdec, sdec = _ret_decay_tables(1)
    head_vec = pl.BlockSpec((H_RET, 1), lambda b: (0, 0))
    row3 = lambda w: pl.BlockSpec((1, H_RET, w), lambda b: (b, 0, 0))
    return pl.pallas_call(
        _ret_sample_kernel,
        out_shape=(SDS((B, H_RET, DV_RET), F32), SDS((B, H_RET, DK_RET, DV_RET), F32)),
        grid=(B,),
        in_specs=[head_vec, head_vec, head_vec, head_vec,
                  pl.BlockSpec((1, DV_RET), lambda b: (0, 0)),
                  row3(DK_RET), row3(DK_RET), row3(DV_RET), row3(DV_RET),
                  pl.BlockSpec((None, 1, H_RET, DK_RET, DV_RET), lambda b: (layer, b, 0, 0, 0))],
        out_specs=(row3(DV_RET), pl.BlockSpec((1, H_RET, DK_RET, DV_RET), lambda b: (b, 0, 0, 0))),
        compiler_params=_cparams(("arbitrary",), 32),
        name="ret_sample",
    )(dmask.reshape(H_RET, 1), qdec.reshape(H_RET, 1), kdec.reshape(H_RET, 1), sdec.reshape(H_RET, 1),
      g_ret.reshape(1, DV_RET), q, k, v, g, state_ret)


def _outproj_kernel(oa_ref, ob_ref, w_ref, x_ref, gt_ref, o_ref, wbf_ref):
    @pl.when(pl.program_id(1) == 0)
    def _cast_weights():
        wbf_ref[...] = w_ref[...].astype(BF16)

    acc = jnp.dot(oa_ref[...].astype(BF16), wbf_ref[0:W_DA, :], preferred_element_type=F32)
    acc += jnp.dot(ob_ref[...].astype(BF16), wbf_ref[W_DA:W_DA + W_RET, :], preferred_element_type=F32)
    o_ref[...] = x_ref[...] + gt_ref[...] * acc


def _outproj(o_da, o_ret, w_out, layer, x, mod, gt_col, tm):
    M, D = x.shape
    tn = 512
    gt_spec = (pl.BlockSpec((1, tn), lambda j, i: (0, gt_col * (D // tn) + j)) if mod.shape[0] == 1
               else pl.BlockSpec((tm, tn), lambda j, i: (i, gt_col * (D // tn) + j)))
    return pl.pallas_call(
        _outproj_kernel,
        out_shape=SDS((M, D), F32),
        grid=(D // tn, M // tm),
        in_specs=[pl.BlockSpec((tm, W_DA), lambda j, i: (i, 0)),
                  pl.BlockSpec((tm, W_RET), lambda j, i: (i, 0)),
                  pl.BlockSpec((None, W_DA + W_RET, tn), lambda j, i: (layer, 0, j)),
                  pl.BlockSpec((tm, tn), lambda j, i: (i, j)),
                  gt_spec],
        out_specs=pl.BlockSpec((tm, tn), lambda j, i: (i, j)),
        scratch_shapes=[pltpu.VMEM((W_DA + W_RET, tn), BF16)],
        compiler_params=_cparams(("arbitrary", "arbitrary"), 56),
        name="outproj",
    )(o_da, o_ret, w_out, x, mod)


def _router_kernel(x_ref, g_ref, sc_ref, sh_ref, wr_ref, br_ref, h_ref, ti_ref, tg_ref):
    h = (_rms(x_ref[...]) * g_ref[...]) * (1.0 + sc_ref[...]) + sh_ref[...]
    h_ref[...] = h
    hb = h.astype(BF16)
    hl = (h - hb.astype(F32)).astype(BF16)
    w = wr_ref[...]
    wb = w.astype(BF16)
    wl = (w - wb.astype(F32)).astype(BF16)
    logits = (jnp.dot(hb, wb, preferred_element_type=F32) + jnp.dot(hb, wl, preferred_element_type=F32)
              + jnp.dot(hl, wb, preferred_element_type=F32)) + br_ref[...]
    lane = lax.broadcasted_iota(I32, logits.shape, 1)
    vals, idxs = [], []
    for _ in range(TOP_K):
        m = jnp.max(logits, axis=-1, keepdims=True)
        idx = jnp.min(jnp.where(logits == m, lane, LANES), axis=-1, keepdims=True)
        vals.append(m)
        idxs.append(idx)
        logits = jnp.where(lane == idx, -jnp.inf, logits)
    exps = [jnp.exp(v - vals[0]) for v in vals]
    denom = exps[0] + exps[1] + exps[2] + exps[3]
    ti = jnp.zeros(lane.shape, I32)
    tg = jnp.zeros(lane.shape, F32)
    for kk in range(TOP_K):
        ti = jnp.where(lane == kk, idxs[kk], ti)
        tg = jnp.where(lane == kk, exps[kk] / denom, tg)
    ti_ref[...] = ti
    tg_ref[...] = tg


def _norm_router(x, g, mod, sh_col, sc_col, wr_pad, br_pad, tm):
    M, D = x.shape
    return pl.pallas_call(
        _router_kernel,
        out_shape=(SDS((M, D), F32), SDS((M, LANES), I32), SDS((M, LANES), F32)),
        grid=(M // tm,),
        in_specs=[pl.BlockSpec((tm, D), lambda i: (i, 0)),
                  pl.BlockSpec((1, D), lambda i: (0, 0)),
                  _mod_spec(mod, tm, sc_col), _mod_spec(mod, tm, sh_col),
                  pl.BlockSpec((D, LANES), lambda i: (0, 0)),
                  pl.BlockSpec((1, LANES), lambda i: (0, 0))],
        out_specs=(pl.BlockSpec((tm, D), lambda i: (i, 0)),
                   pl.BlockSpec((tm, LANES), lambda i: (i, 0)),
                   pl.BlockSpec((tm, LANES), lambda i: (i, 0))),
        compiler_params=_cparams(("arbitrary",), 48),
        name="norm_router",
    )(x, g.reshape(1, D), mod, mod, wr_pad, br_pad)


def _route_tables(top_i, tm):
    T = top_i.shape[0]
    A = T * TOP_K
    experts = jnp.arange(N_EXPERTS, dtype=I32)
    flat_e = top_i.reshape(A)
    chunk = 256
    assert A % chunk == 0
    onehot = (flat_e[:, None] == experts[None, :]).astype(BF16).reshape(A // chunk, chunk, N_EXPERTS)
    lower = jnp.tril(jnp.ones((chunk, chunk), BF16))
    within = jnp.einsum('ij,cjk->cik', lower, onehot, preferred_element_type=F32).astype(I32)
    chunk_tot = within[:, -1, :]
    csum = (within + (jnp.cumsum(chunk_tot, axis=0) - chunk_tot)[:, None, :]).reshape(A, N_EXPERTS)
    rank = jnp.take_along_axis(csum, flat_e[:, None], axis=1)[:, 0] - 1
    counts = csum[-1]
    pcounts = (counts + tm - 1) // tm * tm
    pend = jnp.cumsum(pcounts)
    dest = (pend - pcounts)[flat_e] + rank
    n_blocks = (A + N_EXPERTS * (tm - 1) + tm - 1) // tm
    row_tok = jnp.zeros((n_blocks * tm,), I32).at[dest].set(jnp.arange(A, dtype=I32) // TOP_K)
    n_used = (pend[-1] // tm).astype(I32)
    blk = jnp.arange(n_blocks, dtype=I32)
    blk_e = jnp.minimum(jnp.searchsorted(pend, blk * tm, side='right'), N_EXPERTS - 1).astype(I32)
    blk_e = jnp.where(blk < n_used, blk_e, blk_e[jnp.maximum(n_used - 1, 0)])
    present = jnp.where(counts > 0, experts, N_EXPERTS)
    first_at_or_after = jnp.flip(lax.cummin(jnp.flip(present)))
    next_e = jnp.concatenate([first_at_or_after[1:], jnp.full((1,), N_EXPERTS, I32)])
    next_e = jnp.where(next_e >= N_EXPERTS, -1, next_e)
    return dest.astype(I32), row_tok, blk_e, next_e[blk_e].astype(I32), n_used.reshape(1)


def _start_row_copies(idx_ref, src_hbm, buf_ref, slot, sem, n):
    def body(r, carry):
        pltpu.make_async_copy(src_hbm.at[pl.ds(idx_ref[r], 1), :], buf_ref.at[slot, pl.ds(r, 1), :],
                              sem.at[slot]).start()
        return carry
    lax.fori_loop(0, n, body, 0)


def _wait_row_copies(src_hbm, buf_ref, slot, sem, n):
    def body(r, carry):
        pltpu.make_async_copy(src_hbm.at[pl.ds(0, 1), :], buf_ref.at[slot, pl.ds(r, 1), :],
                              sem.at[slot]).wait()
        return carry
    lax.fori_loop(0, n, body, 0)


def _gather_kernel(nu_ref, tok_ref, tok_next_ref, h_hbm, o_ref, buf_ref, sem, *, rows):
    i = pl.program_id(0)
    slot = i % 2

    @pl.when(i == 0)
    def _prime():
        _start_row_copies(tok_ref, h_hbm, buf_ref, 0, sem, rows)

    @pl.when(i + 1 < nu_ref[0])
    def _fetch_next():
        _start_row_copies(tok_next_ref, h_hbm, buf_ref, 1 - slot, sem, rows)

    @pl.when(i < nu_ref[0])
    def _emit():
        _wait_row_copies(h_hbm, buf_ref, slot, sem, rows)
        o_ref[...] = buf_ref[slot].astype(o_ref.dtype)

    @pl.when(i >= nu_ref[0])
    def _unused_block():
        o_ref[...] = jnp.zeros(o_ref.shape, o_ref.dtype)


def _gather_rows(h, row_tok, n_used, rows):
    P = row_tok.shape[0]
    D = h.shape[1]
    nb = P // rows
    return pl.pallas_call(
        functools.partial(_gather_kernel, rows=rows),
        out_shape=SDS((P, D), BF16),
        grid=(nb,),
        in_specs=[pl.BlockSpec((1,), lambda i: (0,), memory_space=pltpu.SMEM),
                  pl.BlockSpec((rows,), lambda i: (i,), memory_space=pltpu.SMEM),
                  pl.BlockSpec((rows,), lambda i: (jnp.minimum(i + 1, nb - 1),), memory_space=pltpu.SMEM),
                  pl.BlockSpec(memory_space=pl.ANY)],
        out_specs=pl.BlockSpec((rows, D), lambda i: (i, 0)),
        scratch_shapes=[pltpu.VMEM((2, rows, D), F32), pltpu.SemaphoreType.DMA((2,))],
        compiler_params=_cparams(("arbitrary",), 32),
        name="moe_gather",
    )(n_used, row_tok, row_tok, h)


def _new_expert(be_ref, i):
    return (i == 0) | (be_ref[i] != be_ref[jnp.maximum(i - 1, 0)])


def _expert_tile_stream(be_ref, nx_ref, nu_ref, w_hbm, wf_ref, sem, layer, tn, first_block, other_block):
    j = pl.program_id(0)
    i = pl.program_id(1)
    is_first = _new_expert(be_ref, i)

    def tile_copy(e, jj):
        cols = pl.ds(pl.multiple_of(jj * tn, tn), tn)
        return pltpu.make_async_copy(w_hbm.at[layer, e, :, cols], wf_ref, sem)

    @pl.when((j == 0) & (i == 0))
    def _prime():
        tile_copy(be_ref[0], 0).start()

    @pl.when(is_first)
    def _first_block_of_group():
        tile_copy(be_ref[i], j).wait()
        first_block()
        nx = nx_ref[i]

        @pl.when(nx >= 0)
        def _same_tile_next_group():
            tile_copy(nx, j).start()

        @pl.when((nx < 0) & (j + 1 < pl.num_programs(0)))
        def _next_tile_first_group():
            tile_copy(be_ref[0], j + 1).start()

    @pl.when(jnp.logical_not(is_first) & (i < nu_ref[0]))
    def _other_block_of_group():
        other_block()


def _moe_gu_kernel(be_ref, nx_ref, nu_ref, x_ref, w_hbm, b_ref, o_ref, wf_ref, wbf_ref, sem, *, tn, layer):
    def compute(weight_cols):
        x = x_ref[...]
        even = (lax.broadcasted_iota(I32, (x.shape[0], LANES), 1) % 2) == 0
        for c in range(tn // MXU_N):
            cols = slice(c * MXU_N, (c + 1) * MXU_N)
            gu = jnp.dot(x, weight_cols(cols), preferred_element_type=F32) + b_ref[0, :, cols]
            a = gu[:, :LANES]
            b = gu[:, LANES:]
            gate = jnp.where(even, a, pltpu.roll(b, 1, 1))
            up = jnp.where(even, pltpu.roll(a, LANES - 1, 1), b)
            gate = jnp.minimum(gate, SWIGLU_LIMIT)
            up = jnp.clip(up, -SWIGLU_LIMIT, SWIGLU_LIMIT)
            act = (up + 1.0) * gate * jax.nn.sigmoid(SWIGLU_ALPHA * gate)
            o_ref[:, c * LANES:(c + 1) * LANES] = act.astype(o_ref.dtype)

    def cast_and_keep(cols):
        w = wf_ref[:, cols].astype(BF16)
        wbf_ref[:, cols] = w
        return w

    _expert_tile_stream(be_ref, nx_ref, nu_ref, w_hbm, wf_ref, sem, layer, tn,
                        first_block=lambda: compute(cast_and_keep),
                        other_block=lambda: compute(lambda cols: wbf_ref[:, cols]))

    @pl.when(pl.program_id(1) >= nu_ref[0])
    def _unused_block():
        o_ref[...] = jnp.zeros(o_ref.shape, o_ref.dtype)


def _moe_gate_up(xs, w_gu, b_gu, layer, blk_e, blk_next, n_used, tm):
    P, D = xs.shape
    F2 = w_gu.shape[3]
    tn = 1024
    grid_spec = pltpu.PrefetchScalarGridSpec(
        num_scalar_prefetch=3,
        grid=(F2 // tn, P // tm),
        in_specs=[pl.BlockSpec((tm, D), lambda j, i, be, nx, nu: (i, 0)),
                  pl.BlockSpec(memory_space=pl.ANY),
                  pl.BlockSpec((1, 1, tn), lambda j, i, be, nx, nu: (be[i], 0, j))],
        out_specs=pl.BlockSpec((tm, tn // 2), lambda j, i, be, nx, nu: (i, j)),
        scratch_shapes=[pltpu.VMEM((D, tn), F32), pltpu.VMEM((D, tn), BF16), pltpu.SemaphoreType.DMA(())],
    )
    return pl.pallas_call(
        functools.partial(_moe_gu_kernel, tn=tn, layer=layer),
        out_shape=SDS((P, F2 // 2), BF16),
        grid_spec=grid_spec,
        compiler_params=_cparams(("arbitrary", "arbitrary"), 48),
        name="moe_gate_up",
    )(blk_e, blk_next, n_used, xs, w_gu, b_gu[layer].reshape(N_EXPERTS, 1, F2))


def _moe_down_kernel(be_ref, nx_ref, nu_ref, a_ref, w_hbm, b_ref, o_ref, wf_ref, slab_ref, wbf_ref, sem,
                     *, tn, layer):
    half = LANES // 2

    def interleave_and_cast(c):
        cols = slice(c * LANES, (c + 1) * LANES)
        s = c % 2
        for g in range(wf_ref.shape[0] // LANES):
            slab_ref[s, pl.ds(g * LANES, half, stride=2), :] = wf_ref[g * LANES:g * LANES + half, cols]
            slab_ref[s, pl.ds(g * LANES + 1, half, stride=2), :] = (
                wf_ref[g * LANES + half:(g + 1) * LANES, cols])
        wbf_ref[:, cols] = slab_ref[s].astype(BF16)

    def first_block():
        a = a_ref[...]
        for c in range(tn // MXU_N):
            interleave_and_cast(2 * c)
            interleave_and_cast(2 * c + 1)
            cols = slice(c * MXU_N, (c + 1) * MXU_N)
            o_ref[:, cols] = jnp.dot(a, wbf_ref[:, cols], preferred_element_type=F32) + b_ref[0, :, cols]

    def other_block():
        o_ref[...] = jnp.dot(a_ref[...], wbf_ref[...], preferred_element_type=F32) + b_ref[0]

    _expert_tile_stream(be_ref, nx_ref, nu_ref, w_hbm, wf_ref, sem, layer, tn, first_block, other_block)

    @pl.when(pl.program_id(1) >= nu_ref[0])
    def _unused_block():
        o_ref[...] = jnp.zeros(o_ref.shape, o_ref.dtype)


def _moe_down(act, w_down, b_down, layer, blk_e, blk_next, n_used, tm):
    P, F = act.shape
    D = w_down.shape[3]
    tn = 1024
    grid_spec = pltpu.PrefetchScalarGridSpec(
        num_scalar_prefetch=3,
        grid=(D // tn, P // tm),
        in_specs=[pl.BlockSpec((tm, F), lambda j, i, be, nx, nu: (i, 0)),
                  pl.BlockSpec(memory_space=pl.ANY),
                  pl.BlockSpec((1, 1, tn), lambda j, i, be, nx, nu: (be[i], 0, j))],
        out_specs=pl.BlockSpec((tm, tn), lambda j, i, be, nx, nu: (i, j)),
        scratch_shapes=[pltpu.VMEM((F, tn), F32), pltpu.VMEM((2, F, LANES), F32),
                        pltpu.VMEM((F, tn), BF16), pltpu.SemaphoreType.DMA(())],
    )
    return pl.pallas_call(
        functools.partial(_moe_down_kernel, tn=tn, layer=layer),
        out_shape=SDS((P, D), F32),
        grid_spec=grid_spec,
        compiler_params=_cparams(("arbitrary", "arbitrary"), 48),
        name="moe_down",
    )(blk_e, blk_next, n_used, act, w_down, b_down[layer].reshape(N_EXPERTS, 1, D))


def _combine_kernel(dest_ref, dest_next_ref, y_hbm, tg_ref, x_ref, gt_ref, o_ref, buf_ref, sem, *, tb):
    i = pl.program_id(0)
    slot = i % 2
    n = TOP_K * tb

    @pl.when(i == 0)
    def _prime():
        _start_row_copies(dest_ref, y_hbm, buf_ref, 0, sem, n)

    @pl.when(i + 1 < pl.num_programs(0))
    def _fetch_next():
        _start_row_copies(dest_next_ref, y_hbm, buf_ref, 1 - slot, sem, n)

    _wait_row_copies(y_hbm, buf_ref, slot, sem, n)
    tg = tg_ref[...]
    y = buf_ref[slot, 0:tb, :] * tg[:, 0:1]
    for kk in range(1, TOP_K):
        y += buf_ref[slot, kk * tb:(kk + 1) * tb, :] * tg[:, kk:kk + 1]
    o_ref[...] = x_ref[...] + gt_ref[...] * y


def _combine(yb, dest, gates, x, mod, gt_col, tb):
    M, D = x.shape
    nb = M // tb
    n = TOP_K * tb
    dest_by_k = dest.reshape(nb, tb, TOP_K).swapaxes(1, 2).reshape(-1)
    return pl.pallas_call(
        functools.partial(_combine_kernel, tb=tb),
        out_shape=SDS((M, D), F32),
        grid=(nb,),
        in_specs=[pl.BlockSpec((n,), lambda i: (i,), memory_space=pltpu.SMEM),
                  pl.BlockSpec((n,), lambda i: (jnp.minimum(i + 1, nb - 1),), memory_space=pltpu.SMEM),
                  pl.BlockSpec(memory_space=pl.ANY),
                  pl.BlockSpec((tb, LANES), lambda i: (i, 0)),
                  pl.BlockSpec((tb, D), lambda i: (i, 0)),
                  _mod_spec(mod, tb, gt_col)],
        out_specs=pl.BlockSpec((tb, D), lambda i: (i, 0)),
        scratch_shapes=[pltpu.VMEM((2, n, D), F32), pltpu.SemaphoreType.DMA((2,))],
        compiler_params=_cparams(("arbitrary",), 48),
        name="moe_combine",
    )(dest_by_k, dest_by_k, yb, gates, x, mod)


def kernel(x_prompt, x_sample, cache_k, cache_v, state_ret, page_table, c_prompt, c_sample,
           w_ada, b_ada, g_attn, w_in, g_qn, g_kn, lam_q, lam_k, g_sub, g_ret, w_out,
           g_ffn, w_router, b_router, w_gu, b_gu, w_down, b_down):
    depth = w_ada.shape[0]
    assert depth == 1 and x_prompt.shape[0] == 1 and x_sample.shape[1] == 1
    layer = 0
    S = x_prompt.shape[1]
    Bd = x_sample.shape[0]
    D = D_MODEL
    lam_init = 0.8 - 0.6 * math.exp(-0.3 * layer)
    past = page_table.shape[1] * cache_k.shape[2]
    tm_p = 1024
    tm_moe = 256

    xp = x_prompt.reshape(S, D)
    xs = x_sample.reshape(Bd, D)

    n_c = 1 + Bd
    n_c_pad = -(-n_c // 16) * 16
    c_all = jnp.concatenate([c_prompt, c_sample, jnp.zeros((n_c_pad - n_c, D), F32)], axis=0)
    mod = _ada_mod(c_all, w_ada, b_ada, layer)
    mod_p, mod_s = mod[0:1], mod[1:n_c]
    SH1, SC1, GT1, SH2, SC2, GT2 = range(N_ADA)

    tabs_p = _rot_tables(jnp.arange(S))
    tabs_s = _rot_tables(jnp.full((Bd,), past, I32))

    h_p = _norm_mod(xp, g_attn[layer], mod_p, SH1, SC1, 512)
    z_p = _inproj(h_p, w_in, layer, g_qn[layer], g_kn[layer], tabs_p, tm_p)
    o_da_p = _attn_prompt(z_p, lam_q[layer], lam_k[layer], g_sub[layer], lam_init)
    o_ret_p, st_p = _ret_prompt(z_p, g_ret[layer])
    x1_p = _outproj(o_da_p, o_ret_p, w_out, layer, xp, mod_p, GT1, tm_p)

    h_s = _norm_mod(xs, g_attn[layer], mod_s, SH1, SC1, Bd)
    z_s = _inproj(h_s, w_in, layer, g_qn[layer], g_kn[layer], tabs_s, Bd)
    n_maps = 2 * H_DA
    by_map = lambda a: a.reshape(Bd, H_DA, 2, DK_DA).swapaxes(1, 2).reshape(Bd, n_maps, DK_DA)
    o_da_s = _attn_sample(
        by_map(z_s[:, OFF_DQ:OFF_DK]), by_map(z_s[:, OFF_DK:OFF_DV]),
        z_s[:, OFF_DV:OFF_RQ].reshape(Bd, H_DA, DV_DA), cache_k, cache_v, layer, page_table,
        lam_q[layer], lam_k[layer], g_sub[layer], lam_init)
    o_ret_s, st_s = _ret_sample(
        z_s[:, OFF_RQ:OFF_RK].reshape(Bd, H_RET, DK_RET), z_s[:, OFF_RK:OFF_RV].reshape(Bd, H_RET, DK_RET),
        z_s[:, OFF_RV:OFF_RG].reshape(Bd, H_RET, DV_RET), z_s[:, OFF_RG:D_IN].reshape(Bd, H_RET, DV_RET),
        state_ret, layer, g_ret[layer])
    x1_s = _outproj(o_da_s.reshape(Bd, W_DA), o_ret_s.reshape(Bd, W_RET), w_out, layer, xs, mod_s, GT1, Bd)

    wr_pad = jnp.pad(w_router[layer], ((0, 0), (0, LANES - N_EXPERTS)))
    br_pad = jnp.pad(b_router[layer], (0, LANES - N_EXPERTS), constant_values=NEG_INF).reshape(1, LANES)
    h2_p, ti_p, tg_p = _norm_router(x1_p, g_ffn[layer], mod_p, SH2, SC2, wr_pad, br_pad, 512)
    h2_s, ti_s, tg_s = _norm_router(x1_s, g_ffn[layer], mod_s, SH2, SC2, wr_pad, br_pad, Bd)
    h2 = jnp.concatenate([h2_p, h2_s], axis=0)
    top_i = jnp.concatenate([ti_p[:, :TOP_K], ti_s[:, :TOP_K]], axis=0)
    dest, row_tok, blk_e, blk_next, n_used = _route_tables(top_i, tm_moe)
    xg = _gather_rows(h2, row_tok, n_used, tm_moe)
    act = _moe_gate_up(xg, w_gu, b_gu, layer, blk_e, blk_next, n_used, tm_moe)
    yb = _moe_down(act, w_down, b_down, layer, blk_e, blk_next, n_used, tm_moe)
    y_p = _combine(yb, dest[:S * TOP_K], tg_p, x1_p, mod_p, GT2, 128)
    y_s = _combine(yb, dest[S * TOP_K:], tg_s, x1_s, mod_s, GT2, Bd)

    return (y_p.reshape(1, S, D), y_s.reshape(Bd, 1, D),
            z_p[:, OFF_DK:OFF_DV].reshape(1, 1, S, H_DA, 2, DK_DA),
            z_p[:, OFF_DV:OFF_RQ].reshape(1, 1, S, H_DA, DV_DA),
            st_p.reshape(1, 1, H_RET, DK_RET, DV_RET),
            z_s[:, OFF_DK:OFF_DV].reshape(1, Bd, 1, H_DA, 2, DK_DA),
            z_s[:, OFF_DV:OFF_RQ].reshape(1, Bd, 1, H_DA, DV_DA),
            st_s.reshape(1, Bd, H_RET, DK_RET, DV_RET))
```

```python
import functools
import math

import numpy as np
import jax
import jax.numpy as jnp
from jax import lax
from jax.experimental import pallas as pl
from jax.experimental.pallas import tpu as pltpu

F32 = jnp.float32
BF16 = jnp.bfloat16
I32 = jnp.int32
SDS = jax.ShapeDtypeStruct

D_MODEL = 4096
H_DA, DK_DA, DV_DA = 8, 128, 256
H_RET, DK_RET, DV_RET = 8, 128, 256
W_DA = H_DA * DV_DA
W_RET = H_RET * DV_RET
OFF_DQ = 0
OFF_DK = OFF_DQ + H_DA * 2 * DK_DA
OFF_DV = OFF_DK + H_DA * 2 * DK_DA
OFF_RQ = OFF_DV + W_DA
OFF_RK = OFF_RQ + H_RET * DK_RET
OFF_RV = OFF_RK + H_RET * DK_RET
OFF_RG = OFF_RV + W_RET
D_IN = OFF_RG + W_RET
ROPE_THETA = 10000.0
RET_THETA = 10000.0
RET_CHUNK = 128
N_EXPERTS = 32
TOP_K = 4
SWIGLU_LIMIT = 7.0
SWIGLU_ALPHA = 1.702
NORM_EPS = 1e-6
NEG_INF = -1e30
N_ADA = 6
LOG2E = 1.4426950408889634
LANES = 128
MXU_N = 256
MIB = 1024 * 1024

NT_DIMS = (((1,), (1,)), ((), ()))
TN_DIMS = (((0,), (0,)), ((), ()))


def _cparams(sem, vmem_mib):
    return pltpu.CompilerParams(dimension_semantics=sem, vmem_limit_bytes=vmem_mib * MIB)


def _rms(x):
    return x * lax.rsqrt(jnp.mean(x * x, axis=-1, keepdims=True) + NORM_EPS)


def _silu(x):
    return x * jax.nn.sigmoid(x)


def _lane_tile(x, width):
    return jnp.concatenate([x] * (width // LANES), axis=1)


def _ada_kernel(c_ref, w_ref, b_ref, o_ref):
    a = _silu(c_ref[...]).astype(BF16)
    o_ref[...] = jnp.dot(a, w_ref[...].astype(BF16), preferred_element_type=F32) + b_ref[...]


def _ada_mod(c_all, w_ada, b_ada, layer):
    R, D = c_all.shape
    N = w_ada.shape[2]
    tn = 512
    return pl.pallas_call(
        _ada_kernel,
        out_shape=SDS((R, N), F32),
        grid=(N // tn,),
        in_specs=[pl.BlockSpec((R, D), lambda j: (0, 0)),
                  pl.BlockSpec((None, D, tn), lambda j: (layer, 0, j)),
                  pl.BlockSpec((1, tn), lambda j: (layer, j))],
        out_specs=pl.BlockSpec((R, tn), lambda j: (0, j)),
        compiler_params=_cparams(("arbitrary",), 40),
        name="ada_mod",
    )(c_all, w_ada, b_ada)


def _mod_spec(mod, tm, col):
    if mod.shape[0] == 1:
        return pl.BlockSpec((1, D_MODEL), lambda i: (0, col))
    return pl.BlockSpec((tm, D_MODEL), lambda i: (i, col))


def _norm_mod_kernel(x_ref, g_ref, sc_ref, sh_ref, o_ref):
    h = (_rms(x_ref[...]) * g_ref[...]) * (1.0 + sc_ref[...]) + sh_ref[...]
    o_ref[...] = h.astype(o_ref.dtype)


def _norm_mod(x, g, mod, sh_col, sc_col, tm):
    M, D = x.shape
    return pl.pallas_call(
        _norm_mod_kernel,
        out_shape=SDS((M, D), BF16),
        grid=(M // tm,),
        in_specs=[pl.BlockSpec((tm, D), lambda i: (i, 0)),
                  pl.BlockSpec((1, D), lambda i: (0, 0)),
                  _mod_spec(mod, tm, sc_col), _mod_spec(mod, tm, sh_col)],
        out_specs=pl.BlockSpec((tm, D), lambda i: (i, 0)),
        compiler_params=_cparams(("arbitrary",), 48),
        name="norm_mod",
    )(x, g.reshape(1, D), mod, mod)


def _inproj_kernel(h_ref, w_ref, gq_ref, gk_ref, cf_ref, sf_ref, cr_ref, sr_ref, o_ref, wbf_ref, *, tn):
    j = pl.program_id(0)
    i = pl.program_id(1)

    @pl.when(i == 0)
    def _cast_weights():
        wbf_ref[...] = w_ref[...].astype(BF16)

    o_ref[...] = jnp.dot(h_ref[...], wbf_ref[...], preferred_element_type=F32)
    groups = [slice(c * LANES, (c + 1) * LANES) for c in range(tn // LANES)]

    @pl.when(j < OFF_DV // tn)
    def _qk_norm_rope():
        g = jnp.where(j < OFF_DK // tn, gq_ref[...], gk_ref[...])
        cf = cf_ref[...]
        sf = sf_ref[...]
        for sl in groups:
            y = _rms(o_ref[:, sl]) * g
            o_ref[:, sl] = y * cf + pltpu.roll(y, DK_DA // 2, 1) * sf

    @pl.when((j >= OFF_RQ // tn) & (j < OFF_RV // tn))
    def _ret_rotate():
        scale = jnp.where(j >= OFF_RK // tn, DK_RET ** -0.5, 1.0).astype(F32)
        cr = cr_ref[...]
        sr = sr_ref[...]
        even = (lax.broadcasted_iota(I32, cr.shape, 1) % 2) == 0
        for sl in groups:
            x = o_ref[:, sl]
            swapped = jnp.where(even, pltpu.roll(x, LANES - 1, 1), pltpu.roll(x, 1, 1))
            o_ref[:, sl] = (x * cr + swapped * sr) * scale


def _inproj(h, w_in, layer, g_qn, g_kn, tabs, tm):
    M, D = h.shape
    tn = 512
    tab_spec = pl.BlockSpec((tm, LANES), lambda j, i: (i, 0))
    vec_spec = pl.BlockSpec((1, LANES), lambda j, i: (0, 0))
    return pl.pallas_call(
        functools.partial(_inproj_kernel, tn=tn),
        out_shape=SDS((M, D_IN), F32),
        grid=(D_IN // tn, M // tm),
        in_specs=[pl.BlockSpec((tm, D), lambda j, i: (i, 0)),
                  pl.BlockSpec((None, D, tn), lambda j, i: (layer, 0, j)),
                  vec_spec, vec_spec, tab_spec, tab_spec, tab_spec, tab_spec],
        out_specs=pl.BlockSpec((tm, tn), lambda j, i: (i, j)),
        scratch_shapes=[pltpu.VMEM((D, tn), BF16)],
        compiler_params=_cparams(("arbitrary", "arbitrary"), 56),
        name="inproj",
    )(h, w_in, g_qn.reshape(1, LANES), g_kn.reshape(1, LANES), *tabs)


def _rot_tables(pos):
    half = DK_DA // 2
    posf = pos.astype(F32)[:, None]
    inv = ROPE_THETA ** (-jnp.arange(half, dtype=F32) / half)
    ang = posf * inv[None, :]
    cos, sin = jnp.cos(ang), jnp.sin(ang)
    cf = jnp.concatenate([cos, cos], axis=-1)
    sf = jnp.concatenate([-sin, sin], axis=-1)
    inv_r = 1.0 / (RET_THETA ** jnp.linspace(0.0, 1.0, DK_RET // 2, dtype=F32))
    ang_r = posf * inv_r[None, :]
    cos_r, sin_r = jnp.cos(ang_r), jnp.sin(ang_r)
    cr = jnp.repeat(cos_r, 2, axis=-1)
    sr = jnp.stack([-sin_r, sin_r], axis=-1).reshape(pos.shape[0], DK_RET)
    return cf, sf, cr, sr


def _lam_value(lq_ref, lk_ref, lam_init):
    e = jnp.exp(jnp.sum(lq_ref[...] * lk_ref[...], axis=-1, keepdims=True))
    return e[0:1] - e[1:2] + lam_init


def _attn_kernel(qi_ref, kj_ref, lq_ref, lk_ref, gs_ref, q_ref, k_ref, v_ref, o_ref,
                 qbf_ref, m_ref, l_ref, acc_ref, *, tq, qscale, lam_init):
    t = pl.program_id(1)
    qi = qi_ref[t]
    kj = kj_ref[t]

    @pl.when(kj == 0)
    def _init():
        qbf_ref[...] = (q_ref[...] * qscale).astype(BF16)
        m_ref[...] = jnp.full(m_ref.shape, NEG_INF, F32)
        l_ref[...] = jnp.zeros(l_ref.shape, F32)
        acc_ref[...] = jnp.zeros(acc_ref.shape, F32)

    def step(on_diagonal):
        kb = k_ref[...].astype(BF16)
        vb = v_ref[...].astype(BF16)
        if on_diagonal:
            keep = lax.broadcasted_iota(I32, (tq, tq), 1) <= lax.broadcasted_iota(I32, (tq, tq), 0)
        for mm in range(2):
            sl = slice(mm * DK_DA, (mm + 1) * DK_DA)
            s = lax.dot_general(qbf_ref[:, sl], kb[:, sl], NT_DIMS, preferred_element_type=F32)
            if on_diagonal:
                s = jnp.where(keep, s, NEG_INF)
            m_prev = m_ref[mm]
            m_new = jnp.maximum(m_prev, jnp.max(s, axis=-1, keepdims=True))
            alpha = jnp.exp2(m_prev - m_new)
            p = jnp.exp2(s - _lane_tile(m_new, tq))
            l_ref[mm] = l_ref[mm] * alpha + jnp.sum(p, axis=-1, keepdims=True)
            acc_ref[mm] = (acc_ref[mm] * _lane_tile(alpha, DV_DA)
                           + jnp.dot(p.astype(BF16), vb, preferred_element_type=F32))
            m_ref[mm] = m_new

    @pl.when(kj < qi)
    def _below_diagonal():
        step(False)

    @pl.when(kj == qi)
    def _diagonal_and_finish():
        step(True)
        lam = _lam_value(lq_ref, lk_ref, lam_init)
        o = (acc_ref[0] / _lane_tile(l_ref[0], DV_DA)
             - lam * (acc_ref[1] / _lane_tile(l_ref[1], DV_DA)))
        o_ref[...] = (_rms(o) * gs_ref[...] * (1.0 - lam_init)).astype(o_ref.dtype)


def _attn_prompt(z, lam_q, lam_k, g_sub, lam_init):
    S = z.shape[0]
    tq = 512
    nb = S // tq
    wq = 2 * DK_DA
    pairs = [(qi, kj) for qi in range(nb) for kj in range(qi + 1)]
    qi_tab = jnp.asarray(np.array([p[0] for p in pairs], np.int32))
    kj_tab = jnp.asarray(np.array([p[1] for p in pairs], np.int32))
    grid_spec = pltpu.PrefetchScalarGridSpec(
        num_scalar_prefetch=2,
        grid=(H_DA, len(pairs)),
        in_specs=[pl.BlockSpec((2, DK_DA), lambda h, t, qt, kt: (0, 0)),
                  pl.BlockSpec((2, DK_DA), lambda h, t, qt, kt: (0, 0)),
                  pl.BlockSpec((1, DV_DA), lambda h, t, qt, kt: (0, 0)),
                  pl.BlockSpec((tq, wq), lambda h, t, qt, kt: (qt[t], OFF_DQ // wq + h)),
                  pl.BlockSpec((tq, wq), lambda h, t, qt, kt: (kt[t], OFF_DK // wq + h)),
                  pl.BlockSpec((tq, DV_DA), lambda h, t, qt, kt: (kt[t], OFF_DV // DV_DA + h))],
        out_specs=pl.BlockSpec((tq, DV_DA), lambda h, t, qt, kt: (qt[t], h)),
        scratch_shapes=[pltpu.VMEM((tq, wq), BF16),
                        pltpu.VMEM((2, tq, LANES), F32), pltpu.VMEM((2, tq, LANES), F32),
                        pltpu.VMEM((2, tq, DV_DA), F32)],
    )
    return pl.pallas_call(
        functools.partial(_attn_kernel, tq=tq, qscale=DK_DA ** -0.5 * LOG2E, lam_init=lam_init),
        out_shape=SDS((S, W_DA), BF16),
        grid_spec=grid_spec,
        compiler_params=_cparams(("arbitrary", "arbitrary"), 32),
        name="attn_prompt",
    )(qi_tab, kj_tab, lam_q, lam_k, g_sub.reshape(1, DV_DA), z, z, z)


def _attn_s_kernel(pt_ref, lq_ref, lk_ref, gs_ref, q_ref, kn_ref, vn_ref, *rest, pg, scale, lam_init):
    kc_refs = rest[:pg]
    vc_refs = rest[pg:2 * pg]
    o_ref, m_ref, l_ref, acc_ref = rest[2 * pg:]
    step = pl.program_id(1)
    n_maps = 2 * H_DA
    q = q_ref[0]
    qb = q.astype(BF16)
    rowid = lax.broadcasted_iota(I32, (n_maps, 1), 0)

    @pl.when(step == 0)
    def _init_with_new_token():
        m_ref[...] = jnp.sum(q * kn_ref[0], axis=-1, keepdims=True) * scale
        l_ref[...] = jnp.ones(l_ref.shape, F32)
        vn = vn_ref[0]
        acc = jnp.zeros(acc_ref.shape, F32)
        for h in range(H_DA):
            acc = jnp.where(rowid % H_DA == h, vn[h:h + 1, :], acc)
        acc_ref[...] = acc

    page = kc_refs[0].shape[1] // n_maps
    parts = []
    for kc_ref in kc_refs:
        s = jnp.zeros((n_maps, LANES), F32)
        for r in range(n_maps):
            krow = (r % H_DA) * 2 + r // H_DA
            kr = kc_ref[0, pl.ds(krow, page, stride=n_maps), :].astype(BF16)
            s = jnp.where(rowid == r, lax.dot_general(qb, kr, NT_DIMS, preferred_element_type=F32), s)
        parts.append(s)
    s = jnp.concatenate(parts, axis=1) * scale
    m_prev = m_ref[...]
    m_new = jnp.maximum(m_prev, jnp.max(s, axis=-1, keepdims=True))
    alpha = jnp.exp(m_prev - m_new)
    pr = jnp.exp(s - m_new)
    l_ref[...] = l_ref[...] * alpha + jnp.sum(pr, axis=-1, keepdims=True)
    m_ref[...] = m_new
    pb = pr.astype(BF16)
    pv_halves = []
    for half in range(DV_DA // LANES):
        pv = jnp.zeros((n_maps, LANES), F32)
        for h in range(H_DA):
            t = None
            for u, vc_ref in enumerate(vc_refs):
                vh = vc_ref[0, pl.ds(half * H_DA + h, page, stride=n_maps), :].astype(BF16)
                d = jnp.dot(pb[:, u * page:(u + 1) * page], vh, preferred_element_type=F32)
                t = d if t is None else t + d
            pv = jnp.where(rowid % H_DA == h, t, pv)
        pv_halves.append(pv)
    acc_ref[...] = acc_ref[...] * alpha + jnp.concatenate(pv_halves, axis=1)

    @pl.when(step == pl.num_programs(1) - 1)
    def _finish():
        lam = _lam_value(lq_ref, lk_ref, lam_init)
        on = acc_ref[...] / l_ref[...]
        o = on[0:H_DA] - lam * on[H_DA:n_maps]
        o_ref[0] = _rms(o) * gs_ref[...] * (1.0 - lam_init)


def _attn_sample(q, k_new, v_new, cache_k, cache_v, layer, page_table, lam_q, lam_k, g_sub, lam_init):
    B = q.shape[0]
    n_pages = page_table.shape[1]
    depth, n_pool, page = cache_k.shape[0], cache_k.shape[1], cache_k.shape[2]
    assert page == LANES
    n_maps = 2 * H_DA
    pg = 4 if n_pages % 4 == 0 else 1
    first_page = layer * n_pool

    def page_spec(u):
        return pl.BlockSpec((1, page * n_maps, LANES),
                            lambda b, s, pt: (first_page + pt[b * n_pages + s * pg + u], 0, 0))

    grid_spec = pltpu.PrefetchScalarGridSpec(
        num_scalar_prefetch=1,
        grid=(B, n_pages // pg),
        in_specs=[pl.BlockSpec((2, DK_DA), lambda b, s, pt: (0, 0)),
                  pl.BlockSpec((2, DK_DA), lambda b, s, pt: (0, 0)),
                  pl.BlockSpec((1, DV_DA), lambda b, s, pt: (0, 0)),
                  pl.BlockSpec((1, n_maps, DK_DA), lambda b, s, pt: (b, 0, 0)),
                  pl.BlockSpec((1, n_maps, DK_DA), lambda b, s, pt: (b, 0, 0)),
                  pl.BlockSpec((1, H_DA, DV_DA), lambda b, s, pt: (b, 0, 0))]
                 + [page_spec(u) for u in range(pg)] * 2,
        out_specs=pl.BlockSpec((1, H_DA, DV_DA), lambda b, s, pt: (b, 0, 0)),
        scratch_shapes=[pltpu.VMEM((n_maps, 1), F32), pltpu.VMEM((n_maps, 1), F32),
                        pltpu.VMEM((n_maps, DV_DA), F32)],
    )
    kc = cache_k.reshape(depth * n_pool, page * n_maps, DK_DA)
    vc = (cache_v.reshape(depth * n_pool, page, H_DA, DV_DA // LANES, LANES).transpose(0, 1, 3, 2, 4)
          .reshape(depth * n_pool, page * n_maps, LANES))
    return pl.pallas_call(
        functools.partial(_attn_s_kernel, pg=pg, scale=DK_DA ** -0.5, lam_init=lam_init),
        out_shape=SDS((B, H_DA, DV_DA), F32),
        grid_spec=grid_spec,
        compiler_params=_cparams(("arbitrary", "arbitrary"), 40),
        name="attn_sample",
    )(page_table.reshape(-1), lam_q, lam_k, g_sub.reshape(1, DV_DA), q, k_new, v_new,
      *([kc] * pg), *([vc] * pg))


def _ret_decay_tables(chunk):
    lg = jnp.log1p(-jnp.exp2(-5.0 - jnp.arange(H_RET, dtype=F32)))
    idx = jnp.arange(chunk, dtype=F32)
    diff = idx[:, None] - idx[None, :]
    causal = diff >= 0
    dmask = jnp.where(causal, jnp.exp(jnp.where(causal, diff, 0.0)[None] * lg[:, None, None]), 0.0)
    qdec = jnp.exp((idx + 1.0)[None, :] * lg[:, None])[..., None]
    kdec = jnp.exp((chunk - 1.0 - idx)[None, :] * lg[:, None])[..., None]
    sdec = jnp.exp(chunk * lg)[:, None, None]
    return dmask, qdec, kdec, sdec


def _ret_prompt_kernel(dm_ref, qd_ref, kd_ref, sd_ref, gr_ref, q_ref, k_ref, v_ref, g_ref,
                       o_ref, st_ref, state_ref):
    c = pl.program_id(1)

    @pl.when(c == 0)
    def _zero_state():
        state_ref[...] = jnp.zeros(state_ref.shape, F32)

    q = q_ref[...].astype(BF16)
    k = k_ref[...]
    v = v_ref[...].astype(BF16)
    st = state_ref[...]
    qk = lax.dot_general(q, k.astype(BF16), NT_DIMS, preferred_element_type=F32) * dm_ref[0]
    inner = jnp.dot(qk.astype(BF16), v, preferred_element_type=F32)
    cross = jnp.dot(q, st.astype(BF16), preferred_element_type=F32) * qd_ref[0]
    kdecayed = (k * kd_ref[0]).astype(BF16)
    state_ref[...] = sd_ref[0] * st + lax.dot_general(kdecayed, v, TN_DIMS, preferred_element_type=F32)
    o = inner + cross
    o_ref[...] = (_rms(o) * gr_ref[...] * _silu(g_ref[...])).astype(o_ref.dtype)

    @pl.when(c == pl.num_programs(1) - 1)
    def _emit_state():
        st_ref[0] = state_ref[...]


def _ret_prompt(z, g_ret):
    S = z.shape[0]
    C = RET_CHUNK
    dmask, qdec, kdec, sdec = _ret_decay_tables(C)
    per_head = lambda shape: pl.BlockSpec((1,) + shape, lambda h, c: (h, 0, 0))
    return pl.pallas_call(
        _ret_prompt_kernel,
        out_shape=(SDS((S, W_RET), BF16), SDS((H_RET, DK_RET, DV_RET), F32)),
        grid=(H_RET, S // C),
        in_specs=[per_head((C, C)), per_head((C, 1)), per_head((C, 1)), per_head((1, 1)),
                  pl.BlockSpec((1, DV_RET), lambda h, c: (0, 0)),
                  pl.BlockSpec((C, DK_RET), lambda h, c: (c, OFF_RQ // DK_RET + h)),
                  pl.BlockSpec((C, DK_RET), lambda h, c: (c, OFF_RK // DK_RET + h)),
                  pl.BlockSpec((C, DV_RET), lambda h, c: (c, OFF_RV // DV_RET + h)),
                  pl.BlockSpec((C, DV_RET), lambda h, c: (c, OFF_RG // DV_RET + h))],
        out_specs=(pl.BlockSpec((C, DV_RET), lambda h, c: (c, h)),
                   pl.BlockSpec((1, DK_RET, DV_RET), lambda h, c: (h, 0, 0))),
        scratch_shapes=[pltpu.VMEM((DK_RET, DV_RET), F32)],
        compiler_params=_cparams(("arbitrary", "arbitrary"), 32),
        name="ret_prompt",
    )(dmask, qdec, kdec, sdec, g_ret.reshape(1, DV_RET), z, z, z, z)


def _ret_sample_kernel(dm_ref, qd_ref, kd_ref, sd_ref, gr_ref, q_ref, k_ref, v_ref, g_ref, s_ref,
                       o_ref, so_ref):
    q = q_ref[0]
    k = k_ref[0]
    v = v_ref[0]
    rowid = lax.broadcasted_iota(I32, (H_RET, 1), 0)
    qb = q.astype(BF16)
    vb = v.astype(BF16)
    kdecayed = k * kd_ref[...]
    sd = sd_ref[...]
    cross = jnp.zeros(v.shape, F32)
    for h in range(H_RET):
        st = s_ref[0, h]
        cross = jnp.where(rowid == h, jnp.dot(qb, st.astype(BF16), preferred_element_type=F32), cross)
        kh = jnp.where(rowid == h, kdecayed, 0.0).astype(BF16)
        so_ref[0, h] = sd[h:h + 1, :] * st + lax.dot_general(kh, vb, TN_DIMS, preferred_element_type=F32)
    qk = jnp.sum(q * k, axis=-1, keepdims=True) * dm_ref[...]
    o = qk * v + cross * qd_ref[...]
    o_ref[0] = _rms(o) * gr_ref[...] * _silu(g_ref[0])


def _ret_sample(q, k, v, g, state_ret, layer, g_ret):
    B = q.shape[0]
    dmask, qdec, kdec, sdec = _ret_decay_tables(1)
    head_vec = pl.BlockSpec((H_RET, 1), lambda b: (0, 0))
    row3 = lambda w: pl.BlockSpec((1, H_RET, w), lambda b: (b, 0, 0))
    return pl.pallas_call(
        _ret_sample_kernel,
        out_shape=(SDS((B, H_RET, DV_RET), F32), SDS((B, H_RET, DK_RET, DV_RET), F32)),
        grid=(B,),
        in_specs=[head_vec, head_vec, head_vec, head_vec,
                  pl.BlockSpec((1, DV_RET), lambda b: (0, 0)),
                  row3(DK_RET), row3(DK_RET), row3(DV_RET), row3(DV_RET),
                  pl.BlockSpec((None, 1, H_RET, DK_RET, DV_RET), lambda b: (layer, b, 0, 0, 0))],
        out_specs=(row3(DV_RET), pl.BlockSpec((1, H_RET, DK_RET, DV_RET), lambda b: (b, 0, 0, 0))),
        compiler_params=_cparams(("arbitrary",), 32),
        name="ret_sample",
    )(dmask.reshape(H_RET, 1), qdec.reshape(H_RET, 1), kdec.reshape(H_RET, 1), sdec.reshape(H_RET, 1),
      g_ret.reshape(1, DV_RET), q, k, v, g, state_ret)


def _outproj_kernel(oa_ref, ob_ref, w_ref, x_ref, gt_ref, o_ref, wbf_ref):
    @pl.when(pl.program_id(1) == 0)
    def _cast_weights():
        wbf_ref[...] = w_ref[...].astype(BF16)

    acc = jnp.dot(oa_ref[...].astype(BF16), wbf_ref[0:W_DA, :], preferred_element_type=F32)
    acc += jnp.dot(ob_ref[...].astype(BF16), wbf_ref[W_DA:W_DA + W_RET, :], preferred_element_type=F32)
    o_ref[...] = x_ref[...] + gt_ref[...] * acc


def _outproj(o_da, o_ret, w_out, layer, x, mod, gt_col, tm):
    M, D = x.shape
    tn = 512
    gt_spec = (pl.BlockSpec((1, tn), lambda j, i: (0, gt_col * (D // tn) + j)) if mod.shape[0] == 1
               else pl.BlockSpec((tm, tn), lambda j, i: (i, gt_col * (D // tn) + j)))
    return pl.pallas_call(
        _outproj_kernel,
        out_shape=SDS((M, D), F32),
        grid=(D // tn, M // tm),
        in_specs=[pl.BlockSpec((tm, W_DA), lambda j, i: (i, 0)),
                  pl.BlockSpec((tm, W_RET), lambda j, i: (i, 0)),
                  pl.BlockSpec((None, W_DA + W_RET, tn), lambda j, i: (layer, 0, j)),
                  pl.BlockSpec((tm, tn), lambda j, i: (i, j)),
                  gt_spec],
        out_specs=pl.BlockSpec((tm, tn), lambda j, i: (i, j)),
        scratch_shapes=[pltpu.VMEM((W_DA + W_RET, tn), BF16)],
        compiler_params=_cparams(("arbitrary", "arbitrary"), 56),
        name="outproj",
    )(o_da, o_ret, w_out, x, mod)


def _router_kernel(x_ref, g_ref, sc_ref, sh_ref, wr_ref, br_ref, h_ref, ti_ref, tg_ref):
    h = (_rms(x_ref[...]) * g_ref[...]) * (1.0 + sc_ref[...]) + sh_ref[...]
    h_ref[...] = h
    hb = h.astype(BF16)
    hl = (h - hb.astype(F32)).astype(BF16)
    w = wr_ref[...]
    wb = w.astype(BF16)
    wl = (w - wb.astype(F32)).astype(BF16)
    logits = (jnp.dot(hb, wb, preferred_element_type=F32) + jnp.dot(hb, wl, preferred_element_type=F32)
              + jnp.dot(hl, wb, preferred_element_type=F32)) + br_ref[...]
    lane = lax.broadcasted_iota(I32, logits.shape, 1)
    vals, idxs = [], []
    for _ in range(TOP_K):
        m = jnp.max(logits, axis=-1, keepdims=True)
        idx = jnp.min(jnp.where(logits == m, lane, LANES), axis=-1, keepdims=True)
        vals.append(m)
        idxs.append(idx)
        logits = jnp.where(lane == idx, -jnp.inf, logits)
    exps = [jnp.exp(v - vals[0]) for v in vals]
    denom = exps[0] + exps[1] + exps[2] + exps[3]
    ti = jnp.zeros(lane.shape, I32)
    tg = jnp.zeros(lane.shape, F32)
    for kk in range(TOP_K):
        ti = jnp.where(lane == kk, idxs[kk], ti)
        tg = jnp.where(lane == kk, exps[kk] / denom, tg)
    ti_ref[...] = ti
    tg_ref[...] = tg


def _norm_router(x, g, mod, sh_col, sc_col, wr_pad, br_pad, tm):
    M, D = x.shape
    return pl.pallas_call(
        _router_kernel,
        out_shape=(SDS((M, D), F32), SDS((M, LANES), I32), SDS((M, LANES), F32)),
        grid=(M // tm,),
        in_specs=[pl.BlockSpec((tm, D), lambda i: (i, 0)),
                  pl.BlockSpec((1, D), lambda i: (0, 0)),
                  _mod_spec(mod, tm, sc_col), _mod_spec(mod, tm, sh_col),
                  pl.BlockSpec((D, LANES), lambda i: (0, 0)),
                  pl.BlockSpec((1, LANES), lambda i: (0, 0))],
        out_specs=(pl.BlockSpec((tm, D), lambda i: (i, 0)),
                   pl.BlockSpec((tm, LANES), lambda i: (i, 0)),
                   pl.BlockSpec((tm, LANES), lambda i: (i, 0))),
        compiler_params=_cparams(("arbitrary",), 48),
        name="norm_router",
    )(x, g.reshape(1, D), mod, mod, wr_pad, br_pad)


def _route_tables(top_i, tm):
    T = top_i.shape[0]
    A = T * TOP_K
    experts = jnp.arange(N_EXPERTS, dtype=I32)
    flat_e = top_i.reshape(A)
    chunk = 256
    assert A % chunk == 0
    onehot = (flat_e[:, None] == experts[None, :]).astype(BF16).reshape(A // chunk, chunk, N_EXPERTS)
    lower = jnp.tril(jnp.ones((chunk, chunk), BF16))
    within = jnp.einsum('ij,cjk->cik', lower, onehot, preferred_element_type=F32).astype(I32)
    chunk_tot = within[:, -1, :]
    csum = (within + (jnp.cumsum(chunk_tot, axis=0) - chunk_tot)[:, None, :]).reshape(A, N_EXPERTS)
    rank = jnp.take_along_axis(csum, flat_e[:, None], axis=1)[:, 0] - 1
    counts = csum[-1]
    pcounts = (counts + tm - 1) // tm * tm
    pend = jnp.cumsum(pcounts)
    dest = (pend - pcounts)[flat_e] + rank
    n_blocks = (A + N_EXPERTS * (tm - 1) + tm - 1) // tm
    row_tok = jnp.zeros((n_blocks * tm,), I32).at[dest].set(jnp.arange(A, dtype=I32) // TOP_K)
    n_used = (pend[-1] // tm).astype(I32)
    blk = jnp.arange(n_blocks, dtype=I32)
    groups_ended = jnp.sum((pend[None, :] <= (blk * tm)[:, None]).astype(I32), axis=1)
    blk_e = jnp.minimum(groups_ended, N_EXPERTS - 1).astype(I32)
    blk_e = jnp.where(blk < n_used, blk_e, blk_e[jnp.maximum(n_used - 1, 0)])
    present = jnp.where(counts > 0, experts, N_EXPERTS)
    first_at_or_after = jnp.flip(lax.cummin(jnp.flip(present)))
    next_e = jnp.concatenate([first_at_or_after[1:], jnp.full((1,), N_EXPERTS, I32)])
    next_e = jnp.where(next_e >= N_EXPERTS, -1, next_e)
    return dest.astype(I32), row_tok, blk_e, next_e[blk_e].astype(I32), n_used.reshape(1)


def _start_row_copies(idx_ref, src_hbm, buf_ref, slot, sem, n):
    def body(r, carry):
        pltpu.make_async_copy(src_hbm.at[pl.ds(idx_ref[r], 1), :], buf_ref.at[slot, pl.ds(r, 1), :],
                              sem.at[slot]).start()
        return carry
    lax.fori_loop(0, n, body, 0)


def _wait_row_copies(src_hbm, buf_ref, slot, sem, n):
    def body(r, carry):
        pltpu.make_async_copy(src_hbm.at[pl.ds(0, 1), :], buf_ref.at[slot, pl.ds(r, 1), :],
                              sem.at[slot]).wait()
        return carry
    lax.fori_loop(0, n, body, 0)


def _gather_kernel(nu_ref, tok_ref, tok_next_ref, h_hbm, o_ref, buf_ref, sem, *, rows):
    i = pl.program_id(0)
    slot = i % 2

    @pl.when(i == 0)
    def _prime():
        _start_row_copies(tok_ref, h_hbm, buf_ref, 0, sem, rows)

    @pl.when(i + 1 < nu_ref[0])
    def _fetch_next():
        _start_row_copies(tok_next_ref, h_hbm, buf_ref, 1 - slot, sem, rows)

    @pl.when(i < nu_ref[0])
    def _emit():
        _wait_row_copies(h_hbm, buf_ref, slot, sem, rows)
        o_ref[...] = buf_ref[slot].astype(o_ref.dtype)

    @pl.when(i >= nu_ref[0])
    def _unused_block():
        o_ref[...] = jnp.zeros(o_ref.shape, o_ref.dtype)


def _gather_rows(h, row_tok, n_used, rows):
    P = row_tok.shape[0]
    D = h.shape[1]
    nb = P // rows
    return pl.pallas_call(
        functools.partial(_gather_kernel, rows=rows),
        out_shape=SDS((P, D), BF16),
        grid=(nb,),
        in_specs=[pl.BlockSpec((1,), lambda i: (0,), memory_space=pltpu.SMEM),
                  pl.BlockSpec((rows,), lambda i: (i,), memory_space=pltpu.SMEM),
                  pl.BlockSpec((rows,), lambda i: (jnp.minimum(i + 1, nb - 1),), memory_space=pltpu.SMEM),
                  pl.BlockSpec(memory_space=pl.ANY)],
        out_specs=pl.BlockSpec((rows, D), lambda i: (i, 0)),
        scratch_shapes=[pltpu.VMEM((2, rows, D), F32), pltpu.SemaphoreType.DMA((2,))],
        compiler_params=_cparams(("arbitrary",), 32),
        name="moe_gather",
    )(n_used, row_tok, row_tok, h)


def _new_expert(be_ref, i):
    return (i == 0) | (be_ref[i] != be_ref[jnp.maximum(i - 1, 0)])


def _expert_tile_stream(be_ref, nx_ref, nu_ref, w_hbm, wf_ref, sem, cnt_ref, layer, tn, first_block, other_block):
    j = pl.program_id(0)
    i = pl.program_id(1)
    is_first = _new_expert(be_ref, i)

    def tile_copy(e, jj, slot):
        cols = pl.ds(pl.multiple_of(jj * tn, tn), tn)
        return pltpu.make_async_copy(w_hbm.at[layer, e, :, cols], wf_ref.at[slot], sem.at[slot])

    @pl.when((j == 0) & (i == 0))
    def _prime():
        cnt_ref[0] = 0
        tile_copy(be_ref[0], 0, 0).start()

    @pl.when(is_first)
    def _first_block_of_group():
        slot = cnt_ref[0] % 2
        cnt_ref[0] = cnt_ref[0] + 1
        nx = nx_ref[i]

        @pl.when(nx >= 0)
        def _same_tile_next_group():
            tile_copy(nx, j, 1 - slot).start()

        @pl.when((nx < 0) & (j + 1 < pl.num_programs(0)))
        def _next_tile_first_group():
            tile_copy(be_ref[0], j + 1, 1 - slot).start()

        tile_copy(be_ref[i], j, slot).wait()
        first_block(wf_ref.at[slot])

    @pl.when(jnp.logical_not(is_first) & (i < nu_ref[0]))
    def _other_block_of_group():
        other_block()


def _moe_gu_kernel(be_ref, nx_ref, nu_ref, x_ref, w_hbm, b_ref, o_ref, wf_ref, wbf_ref, sem, cnt_ref,
                   *, tn, layer):
    def compute(weight_cols):
        x = x_ref[...]
        even = (lax.broadcasted_iota(I32, (x.shape[0], LANES), 1) % 2) == 0
        for c in range(tn // MXU_N):
            cols = slice(c * MXU_N, (c + 1) * MXU_N)
            gu = jnp.dot(x, weight_cols(cols), preferred_element_type=F32) + b_ref[0, :, cols]
            a = gu[:, :LANES]
            b = gu[:, LANES:]
            gate = jnp.where(even, a, pltpu.roll(b, 1, 1))
            up = jnp.where(even, pltpu.roll(a, LANES - 1, 1), b)
            gate = jnp.minimum(gate, SWIGLU_LIMIT)
            up = jnp.clip(up, -SWIGLU_LIMIT, SWIGLU_LIMIT)
            act = (up + 1.0) * gate * jax.nn.sigmoid(SWIGLU_ALPHA * gate)
            o_ref[:, c * LANES:(c + 1) * LANES] = act.astype(o_ref.dtype)

    def first_block(slab_ref):
        def cast_and_keep(cols):
            w = slab_ref[:, cols].astype(BF16)
            wbf_ref[:, cols] = w
            return w
        compute(cast_and_keep)

    _expert_tile_stream(be_ref, nx_ref, nu_ref, w_hbm, wf_ref, sem, cnt_ref, layer, tn,
                        first_block=first_block,
                        other_block=lambda: compute(lambda cols: wbf_ref[:, cols]))

    @pl.when(pl.program_id(1) >= nu_ref[0])
    def _unused_block():
        o_ref[...] = jnp.zeros(o_ref.shape, o_ref.dtype)


def _moe_gate_up(xs, w_gu, b_gu, layer, blk_e, blk_next, n_used, tm):
    P, D = xs.shape
    F2 = w_gu.shape[3]
    tn = 1024
    grid_spec = pltpu.PrefetchScalarGridSpec(
        num_scalar_prefetch=3,
        grid=(F2 // tn, P // tm),
        in_specs=[pl.BlockSpec((tm, D), lambda j, i, be, nx, nu: (i, 0)),
                  pl.BlockSpec(memory_space=pl.ANY),
                  pl.BlockSpec((1, 1, tn), lambda j, i, be, nx, nu: (be[i], 0, j))],
        out_specs=pl.BlockSpec((tm, tn // 2), lambda j, i, be, nx, nu: (i, j)),
        scratch_shapes=[pltpu.VMEM((2, D, tn), F32), pltpu.VMEM((D, tn), BF16),
                        pltpu.SemaphoreType.DMA((2,)), pltpu.SMEM((1,), I32)],
    )
    return pl.pallas_call(
        functools.partial(_moe_gu_kernel, tn=tn, layer=layer),
        out_shape=SDS((P, F2 // 2), BF16),
        grid_spec=grid_spec,
        compiler_params=_cparams(("arbitrary", "arbitrary"), 56),
        name="moe_gate_up",
    )(blk_e, blk_next, n_used, xs, w_gu, b_gu[layer].reshape(N_EXPERTS, 1, F2))


def _moe_down_kernel(be_ref, nx_ref, nu_ref, a_ref, w_hbm, b_ref, o_ref, wf_ref, slab_ref, wbf_ref, sem,
                     cnt_ref, *, tn, layer):
    half = LANES // 2

    def interleave_and_cast(w_ref, c):
        cols = slice(c * LANES, (c + 1) * LANES)
        s = c % 2
        for g in range(w_ref.shape[0] // LANES):
            slab_ref[s, pl.ds(g * LANES, half, stride=2), :] = w_ref[g * LANES:g * LANES + half, cols]
            slab_ref[s, pl.ds(g * LANES + 1, half, stride=2), :] = (
                w_ref[g * LANES + half:(g + 1) * LANES, cols])
        wbf_ref[:, cols] = slab_ref[s].astype(BF16)

    def first_block(w_ref):
        a = a_ref[...]
        for c in range(tn // MXU_N):
            interleave_and_cast(w_ref, 2 * c)
            interleave_and_cast(w_ref, 2 * c + 1)
            cols = slice(c * MXU_N, (c + 1) * MXU_N)
            o_ref[:, cols] = jnp.dot(a, wbf_ref[:, cols], preferred_element_type=F32) + b_ref[0, :, cols]

    def other_block():
        o_ref[...] = jnp.dot(a_ref[...], wbf_ref[...], preferred_element_type=F32) + b_ref[0]

    _expert_tile_stream(be_ref, nx_ref, nu_ref, w_hbm, wf_ref, sem, cnt_ref, layer, tn,
                        first_block, other_block)

    @pl.when(pl.program_id(1) >= nu_ref[0])
    def _unused_block():
        o_ref[...] = jnp.zeros(o_ref.shape, o_ref.dtype)


def _moe_down(act, w_down, b_down, layer, blk_e, blk_next, n_used, tm):
    P, F = act.shape
    D = w_down.shape[3]
    tn = 1024
    grid_spec = pltpu.PrefetchScalarGridSpec(
        num_scalar_prefetch=3,
        grid=(D // tn, P // tm),
        in_specs=[pl.BlockSpec((tm, F), lambda j, i, be, nx, nu: (i, 0)),
                  pl.BlockSpec(memory_space=pl.ANY),
                  pl.BlockSpec((1, 1, tn), lambda j, i, be, nx, nu: (be[i], 0, j))],
        out_specs=pl.BlockSpec((tm, tn), lambda j, i, be, nx, nu: (i, j)),
        scratch_shapes=[pltpu.VMEM((2, F, tn), F32), pltpu.VMEM((2, F, LANES), F32),
                        pltpu.VMEM((F, tn), BF16), pltpu.SemaphoreType.DMA((2,)), pltpu.SMEM((1,), I32)],
    )
    return pl.pallas_call(
        functools.partial(_moe_down_kernel, tn=tn, layer=layer),
        out_shape=SDS((P, D), F32),
        grid_spec=grid_spec,
        compiler_params=_cparams(("arbitrary", "arbitrary"), 58),
        name="moe_down",
    )(blk_e, blk_next, n_used, act, w_down, b_down[layer].reshape(N_EXPERTS, 1, D))


def _combine_kernel(dest_ref, dest_next_ref, y_hbm, tg_ref, x_ref, gt_ref, o_ref, buf_ref, sem, *, tb):
    i = pl.program_id(0)
    slot = i % 2
    n = TOP_K * tb

    @pl.when(i == 0)
    def _prime():
        _start_row_copies(dest_ref, y_hbm, buf_ref, 0, sem, n)

    @pl.when(i + 1 < pl.num_programs(0))
    def _fetch_next():
        _start_row_copies(dest_next_ref, y_hbm, buf_ref, 1 - slot, sem, n)

    _wait_row_copies(y_hbm, buf_ref, slot, sem, n)
    tg = tg_ref[...]
    y = buf_ref[slot, 0:tb, :] * tg[:, 0:1]
    for kk in range(1, TOP_K):
        y += buf_ref[slot, kk * tb:(kk + 1) * tb, :] * tg[:, kk:kk + 1]
    o_ref[...] = x_ref[...] + gt_ref[...] * y


def _combine(yb, dest, gates, x, mod, gt_col, tb):
    M, D = x.shape
    nb = M // tb
    n = TOP_K * tb
    dest_by_k = dest.reshape(nb, tb, TOP_K).swapaxes(1, 2).reshape(-1)
    return pl.pallas_call(
        functools.partial(_combine_kernel, tb=tb),
        out_shape=SDS((M, D), F32),
        grid=(nb,),
        in_specs=[pl.BlockSpec((n,), lambda i: (i,), memory_space=pltpu.SMEM),
                  pl.BlockSpec((n,), lambda i: (jnp.minimum(i + 1, nb - 1),), memory_space=pltpu.SMEM),
                  pl.BlockSpec(memory_space=pl.ANY),
                  pl.BlockSpec((tb, LANES), lambda i: (i, 0)),
                  pl.BlockSpec((tb, D), lambda i: (i, 0)),
                  _mod_spec(mod, tb, gt_col)],
        out_specs=pl.BlockSpec((tb, D), lambda i: (i, 0)),
        scratch_shapes=[pltpu.VMEM((2, n, D), F32), pltpu.SemaphoreType.DMA((2,))],
        compiler_params=_cparams(("arbitrary",), 48),
        name="moe_combine",
    )(dest_by_k, dest_by_k, yb, gates, x, mod)


def kernel(x_prompt, x_sample, cache_k, cache_v, state_ret, page_table, c_prompt, c_sample,
           w_ada, b_ada, g_attn, w_in, g_qn, g_kn, lam_q, lam_k, g_sub, g_ret, w_out,
           g_ffn, w_router, b_router, w_gu, b_gu, w_down, b_down):
    depth = w_ada.shape[0]
    assert depth == 1 and x_prompt.shape[0] == 1 and x_sample.shape[1] == 1
    layer = 0
    S = x_prompt.shape[1]
    Bd = x_sample.shape[0]
    D = D_MODEL
    lam_init = 0.8 - 0.6 * math.exp(-0.3 * layer)
    past = page_table.shape[1] * cache_k.shape[2]
    tm_p = 1024
    tm_moe = 256

    xp = x_prompt.reshape(S, D)
    xs = x_sample.reshape(Bd, D)

    n_c = 1 + Bd
    n_c_pad = -(-n_c // 16) * 16
    c_all = jnp.concatenate([c_prompt, c_sample, jnp.zeros((n_c_pad - n_c, D), F32)], axis=0)
    mod = _ada_mod(c_all, w_ada, b_ada, layer)
    mod_p, mod_s = mod[0:1], mod[1:n_c]
    SH1, SC1, GT1, SH2, SC2, GT2 = range(N_ADA)

    tabs_p = _rot_tables(jnp.arange(S))
    tabs_s = _rot_tables(jnp.full((Bd,), past, I32))

    h_p = _norm_mod(xp, g_attn[layer], mod_p, SH1, SC1, 512)
    z_p = _inproj(h_p, w_in, layer, g_qn[layer], g_kn[layer], tabs_p, tm_p)
    o_da_p = _attn_prompt(z_p, lam_q[layer], lam_k[layer], g_sub[layer], lam_init)
    o_ret_p, st_p = _ret_prompt(z_p, g_ret[layer])
    x1_p = _outproj(o_da_p, o_ret_p, w_out, layer, xp, mod_p, GT1, tm_p)

    h_s = _norm_mod(xs, g_attn[layer], mod_s, SH1, SC1, Bd)
    z_s = _inproj(h_s, w_in, layer, g_qn[layer], g_kn[layer], tabs_s, Bd)
    n_maps = 2 * H_DA
    by_map = lambda a: a.reshape(Bd, H_DA, 2, DK_DA).swapaxes(1, 2).reshape(Bd, n_maps, DK_DA)
    o_da_s = _attn_sample(
        by_map(z_s[:, OFF_DQ:OFF_DK]), by_map(z_s[:, OFF_DK:OFF_DV]),
        z_s[:, OFF_DV:OFF_RQ].reshape(Bd, H_DA, DV_DA), cache_k, cache_v, layer, page_table,
        lam_q[layer], lam_k[layer], g_sub[layer], lam_init)
    o_ret_s, st_s = _ret_sample(
        z_s[:, OFF_RQ:OFF_RK].reshape(Bd, H_RET, DK_RET), z_s[:, OFF_RK:OFF_RV].reshape(Bd, H_RET, DK_RET),
        z_s[:, OFF_RV:OFF_RG].reshape(Bd, H_RET, DV_RET), z_s[:, OFF_RG:D_IN].reshape(Bd, H_RET, DV_RET),
        state_ret, layer, g_ret[layer])
    x1_s = _outproj(o_da_s.reshape(Bd, W_DA), o_ret_s.reshape(Bd, W_RET), w_out, layer, xs, mod_s, GT1, Bd)

    wr_pad = jnp.pad(w_router[layer], ((0, 0), (0, LANES - N_EXPERTS)))
    br_pad = jnp.pad(b_router[layer], (0, LANES - N_EXPERTS), constant_values=NEG_INF).reshape(1, LANES)
    h2_p, ti_p, tg_p = _norm_router(x1_p, g_ffn[layer], mod_p, SH2, SC2, wr_pad, br_pad, 512)
    h2_s, ti_s, tg_s = _norm_router(x1_s, g_ffn[layer], mod_s, SH2, SC2, wr_pad, br_pad, Bd)
    h2 = jnp.concatenate([h2_p, h2_s], axis=0)
    top_i = jnp.concatenate([ti_p[:, :TOP_K], ti_s[:, :TOP_K]], axis=0)
    dest, row_tok, blk_e, blk_next, n_used = _route_tables(top_i, tm_moe)
    xg = _gather_rows(h2, row_tok, n_used, tm_moe)
    act = _moe_gate_up(xg, w_gu, b_gu, layer, blk_e, blk_next, n_used, tm_moe)
    yb = _moe_down(act, w_down, b_down, layer, blk_e, blk_next, n_used, tm_moe)
    y_p = _combine(yb, dest[:S * TOP_K], tg_p, x1_p, mod_p, GT2, 128)
    y_s = _combine(yb, dest[S * TOP_K:], tg_s, x1_s, mod_s, GT2, Bd)

    return (y_p.reshape(1, S, D), y_s.reshape(Bd, 1, D),
            z_p[:, OFF_DK:OFF_DV].reshape(1, 1, S, H_DA, 2, DK_DA),
            z_p[:, OFF_DV:OFF_RQ].reshape(1, 1, S, H_DA, DV_DA),
            st_p.reshape(1, 1, H_RET, DK_RET, DV_RET),
            z_s[:, OFF_DK:OFF_DV].reshape(1, Bd, 1, H_DA, 2, DK_DA),
            z_s[:, OFF_DV:OFF_RQ].reshape(1, Bd, 1, H_DA, DV_DA),
            st_s.reshape(1, Bd, H_RET, DK_RET, DV_RET))
```
